```python
import jax, jax.numpy as jnp
from jax import lax
import numpy as np

D_MODEL = 1024
BATCH = 32
SEQ = 256
DEPTH = 1
DEC_BATCH = 2
DEC_SEQ = 4096
PAST_LEN = 256

GRID_W = 64
WIDTH_A = D_MODEL // 2
HEAD_A = 128
N_HEADS_A = WIDTH_A // HEAD_A
WIDTH_B = D_MODEL - WIDTH_A
HEAD_B = 64
N_HEADS_B = WIDTH_B // HEAD_B
LORA_W = 32
LORA_A = 32
LORA_G = 96
CHUNK = 64
D_FF = 2816
CONV_W = 3
RMS_EPS = 1e-6
GN_EPS = 64e-5
DECAY_SCALE = 0.6065306597
P_A = 5 * WIDTH_A
P_B = 3 * WIDTH_B + 2 * LORA_W + LORA_A + LORA_G
P_IN = P_A + P_B

kernel_name = 'hymba_hgrn2_rwkv7_convffn_diffusion_step'


def rmsnorm(x, w):
    xf = x.astype(jnp.float32)
    y = xf * lax.rsqrt(jnp.mean(xf * xf, axis=-1, keepdims=True) + RMS_EPS)
    return (y * w.astype(jnp.float32)).astype(x.dtype)


def dwconv1d(x, w):
    C = x.shape[-1]
    return lax.conv_general_dilated(x, w[:, None, :].astype(x.dtype), (1,), 'SAME',
                                    dimension_numbers=('NWC', 'WIO', 'NWC'), feature_group_count=C)


def dwconv2d_grid(x, w):
    B, T, C = x.shape
    rows = T // GRID_W
    y = lax.conv_general_dilated(x.reshape(B, rows, GRID_W, C), w[:, :, None, :].astype(x.dtype), (1, 1), 'SAME',
                                 dimension_numbers=('NHWC', 'HWIO', 'NHWC'), feature_group_count=C)
    return y.reshape(B, T, C)


def hgrn2_chunk_scan(q, k, v, logf, s0):
    B, T, H, DK = q.shape
    DV = v.shape[-1]
    nc = T // CHUNK

    def chunks(z):
        return z.reshape(B, nc, CHUNK, H, z.shape[-1]).transpose(1, 0, 3, 2, 4)

    causal = jnp.tril(jnp.ones((CHUNK, CHUNK), dtype=bool))

    def step(S, inp):
        qc, kc, vc, gc = inp
        b = jnp.cumsum(gc, axis=-2)
        b_last = b[:, :, -1:, :]
        inter = jnp.einsum('bhtd,bhde->bhte', qc * jnp.exp(b), S)
        diff = jnp.where(causal[:, :, None], b[:, :, :, None, :] - b[:, :, None, :, :], -jnp.inf)
        scores = jnp.einsum('bhtd,bhsd,bhtsd->bhts', qc, kc, jnp.exp(diff))
        intra = jnp.einsum('bhts,bhse->bhte', scores, vc)
        S = jnp.exp(b_last)[:, :, 0, :, None] * S + jnp.einsum('bhsd,bhse->bhde', kc * jnp.exp(b_last - b), vc)
        return S, inter + intra

    S, o = lax.scan(step, s0.astype(q.dtype), tuple(chunks(z) for z in (q, k, v, logf)))
    o = o.transpose(1, 0, 3, 2, 4).reshape(B, T, H, DV)
    return o, S


def hgrn2_mixer(u_a, lb, norm_w, s0):
    B, T, _ = u_a.shape
    q, i, zf, zb, g = jnp.split(u_a, 5, axis=-1)
    heads = lambda z: z.reshape(B, T, N_HEADS_A, HEAD_A)
    flip = lambda z: jnp.flip(z, axis=1)
    q = heads(jax.nn.silu(q))
    i = heads(i)
    f_f = lb[0] + (1 - lb[0]) * jax.nn.sigmoid(zf)
    f_b = lb[1] + (1 - lb[1]) * jax.nn.sigmoid(zb)
    o_f, S_f = hgrn2_chunk_scan(q, heads(1 - f_f), i, heads(jnp.log(f_f)), s0[:, 0])
    o_b, S_b = hgrn2_chunk_scan(flip(q), flip(heads(1 - f_b)), flip(i), flip(heads(jnp.log(f_b))), s0[:, 1])
    o = o_f + flip(o_b)
    o = rmsnorm(o, norm_w.reshape(N_HEADS_A, HEAD_A)).reshape(B, T, WIDTH_A)
    return o * jax.nn.silu(g), jnp.stack([S_f, S_b], axis=1)


def rwkv7_scan(r, w, k, v, kk, a, s0):
    xs = tuple(jnp.moveaxis(z, 1, 0) for z in (r, w, k, v, kk, a))

    def step(S, inp):
        r_t, w_t, k_t, v_t, kk_t, a_t = inp
        sk = jnp.einsum('bhij,bhj->bhi', S, kk_t)
        S = (S * w_t[:, :, None, :] - sk[..., None] * (kk_t * a_t)[:, :, None, :]
             + v_t[..., None] * k_t[:, :, None, :])
        return S, jnp.einsum('bhij,bhj->bhi', S, r_t)

    S, ys = lax.scan(step, s0.astype(r.dtype), xs)
    return jnp.moveaxis(ys, 0, 1), S


def rwkv7_mixer(u_b, p, s0):
    B, T, _ = u_b.shape
    u_b = dwconv1d(u_b, p['rwkv_conv'])
    idx = [WIDTH_B, 2 * WIDTH_B, 3 * WIDTH_B, 3 * WIDTH_B + LORA_W, 3 * WIDTH_B + 2 * LORA_W,
           3 * WIDTH_B + 2 * LORA_W + LORA_A]
    r, k, v, wdf, wdb, ad, gd = jnp.split(u_b, idx, axis=-1)
    heads = lambda z: z.reshape(B, T, N_HEADS_B, HEAD_B)
    flip = lambda z: jnp.flip(z, axis=1)
    w_f = jnp.exp(-DECAY_SCALE * jax.nn.sigmoid(p['rwkv_w0'][0] + jnp.tanh(wdf) @ p['rwkv_w2'][0]))
    w_b = jnp.exp(-DECAY_SCALE * jax.nn.sigmoid(p['rwkv_w0'][1] + jnp.tanh(wdb) @ p['rwkv_w2'][1]))
    a = jax.nn.sigmoid(p['rwkv_a0'] + ad @ p['rwkv_a2'])
    g = jax.nn.sigmoid(gd) @ p['rwkv_g2']
    kk = heads(k * p['rwkv_k_k']).astype(jnp.float32)
    kk = (kk * lax.rsqrt(jnp.sum(kk * kk, axis=-1, keepdims=True) + 1e-12)).astype(u_b.dtype)
    k = k * (1 + (a - 1) * p['rwkv_k_a'])
    r, k, v, a = heads(r), heads(k), heads(v), heads(a)
    w_f, w_b = heads(w_f), heads(w_b)
    y_f, S_f = rwkv7_scan(r, w_f, k, v, kk, a, s0[:, 0])
    y_b, S_b = rwkv7_scan(flip(r), flip(w_b), flip(k), flip(v), flip(kk), flip(a), s0[:, 1])
    y = (y_f + flip(y_b)).astype(jnp.float32)
    mu = jnp.mean(y, axis=-1, keepdims=True)
    var = jnp.mean(jnp.square(y - mu), axis=-1, keepdims=True)
    yn = (y - mu) * lax.rsqrt(var + GN_EPS)
    yn = (yn * p['rwkv_ln_w'].reshape(N_HEADS_B, HEAD_B) + p['rwkv_ln_b'].reshape(N_HEADS_B, HEAD_B)).astype(u_b.dtype)
    bonus = jnp.sum(r * k * p['rwkv_r_k'].reshape(N_HEADS_B, HEAD_B), axis=-1, keepdims=True) * v
    out = (yn + bonus).reshape(B, T, WIDTH_B) * g
    return out, jnp.stack([S_f, S_b], axis=1)


def block(x, mod, s_hgrn, s_rwkv, ffn_conv_fn, p):
    shift1, scale1, gate1, shift2, scale2, gate2 = jnp.split(mod[:, None, :], 6, axis=-1)
    h = rmsnorm(x, p['norm_mix_w']) * (1 + scale1) + shift1
    u = h @ p['w_in']
    o_a, s_hgrn_new = hgrn2_mixer(u[..., :P_A], p['lb'], p['hgrn_norm_w'], s_hgrn)
    o_b, s_rwkv_new = rwkv7_mixer(u[..., P_A:], p, s_rwkv)
    x = x + gate1 * (jnp.concatenate([o_a, o_b], axis=-1) @ p['w_out'])
    h = rmsnorm(x, p['norm_ffn_w']) * (1 + scale2) + shift2
    gt = ffn_conv_fn(h @ p['ffn_w_gate']) + p['ffn_conv_b']
    x = x + gate2 * ((jax.nn.gelu(gt) * (h @ p['ffn_w_up'])) @ p['ffn_w_down'])
    return x, s_hgrn_new, s_rwkv_new


def setup_inputs(seed: int = 0) -> dict:
    key = jax.random.key(seed)
    ks = jax.random.split(key, 32)
    nrm = lambda k, shape, s: jax.random.normal(k, shape, jnp.float32) * s
    L = DEPTH
    centre = jnp.eye(CONV_W, dtype=jnp.float32)[1]
    return {
        'x_prompt': nrm(ks[0], (BATCH, SEQ, D_MODEL), 1.0),
        'x_sample': nrm(ks[1], (DEC_BATCH, DEC_SEQ, D_MODEL), 1.0),
        'state_hgrn': nrm(ks[2], (DEC_BATCH, L, 2, N_HEADS_A, HEAD_A, HEAD_A), 0.5),
        'state_rwkv': nrm(ks[3], (DEC_BATCH, L, 2, N_HEADS_B, HEAD_B, HEAD_B), 0.5),
        'c': nrm(ks[4], (DEC_BATCH, D_MODEL), 1.0),
        'c_ctx': nrm(ks[5], (D_MODEL,), 1.0),
        'ada_w': nrm(ks[6], (L, D_MODEL, 6 * D_MODEL), 0.5 * D_MODEL ** -0.5),
        'ada_b': nrm(ks[7], (L, 6 * D_MODEL), 0.02),
        'norm_mix_w': 1.0 + nrm(ks[8], (L, D_MODEL), 0.05),
        'w_in': nrm(ks[9], (L, D_MODEL, P_IN), D_MODEL ** -0.5),
        'hgrn_lb': nrm(ks[10], (L + 1, 2, WIDTH_A), 0.5),
        'hgrn_norm_w': 1.0 + nrm(ks[11], (L, WIDTH_A), 0.05),
        'rwkv_conv': nrm(ks[12], (L, CONV_W, P_B), 0.3) + centre[None, :, None],
        'rwkv_w0': nrm(ks[13], (L, 2, WIDTH_B), 1.0),
        'rwkv_w2': nrm(ks[14], (L, 2, LORA_W, WIDTH_B), 0.3 * LORA_W ** -0.5),
        'rwkv_a0': nrm(ks[15], (L, WIDTH_B), 0.5),
        'rwkv_a2': nrm(ks[16], (L, LORA_A, WIDTH_B), 0.5 * LORA_A ** -0.5),
        'rwkv_g2': nrm(ks[17], (L, LORA_G, WIDTH_B), LORA_G ** -0.5),
        'rwkv_k_k': 1.0 + nrm(ks[18], (L, WIDTH_B), 0.1),
        'rwkv_k_a': 1.0 + nrm(ks[19], (L, WIDTH_B), 0.1),
        'rwkv_r_k': nrm(ks[20], (L, WIDTH_B), 0.1),
        'rwkv_ln_w': 1.0 + nrm(ks[21], (L, WIDTH_B), 0.05),
        'rwkv_ln_b': nrm(ks[22], (L, WIDTH_B), 0.01),
        'w_out': nrm(ks[23], (L, D_MODEL, D_MODEL), D_MODEL ** -0.5),
        'norm_ffn_w': 1.0 + nrm(ks[24], (L, D_MODEL), 0.05),
        'ffn_w_gate': nrm(ks[25], (L, D_MODEL, D_FF), D_MODEL ** -0.5),
        'ffn_w_up': nrm(ks[26], (L, D_MODEL, D_FF), D_MODEL ** -0.5),
        'ffn_conv': nrm(ks[27], (L, CONV_W, CONV_W, D_FF), 1.0 / 3.0),
        'ffn_conv_b': nrm(ks[28], (L, D_FF), 0.02),
        'ffn_w_down': nrm(ks[29], (L, D_FF, D_MODEL), D_FF ** -0.5),
        'final_norm_w': 1.0 + nrm(ks[30], (D_MODEL,), 0.05),
    }


def reference(x_prompt, x_sample, state_hgrn, state_rwkv, c, c_ctx, ada_w, ada_b, norm_mix_w, w_in,
              hgrn_lb, hgrn_norm_w, rwkv_conv, rwkv_w0, rwkv_w2, rwkv_a0, rwkv_a2, rwkv_g2, rwkv_k_k,
              rwkv_k_a, rwkv_r_k, rwkv_ln_w, rwkv_ln_b, w_out, norm_ffn_w, ffn_w_gate, ffn_w_up, ffn_conv,
              ffn_conv_b, ffn_w_down, final_norm_w):
    lb_all = jnp.cumsum(jax.nn.softmax(hgrn_lb.astype(jnp.float32), axis=0), axis=0).astype(x_prompt.dtype)
    b_ctx = x_prompt.shape[0]
    zeros_h = jnp.zeros((b_ctx, 2, N_HEADS_A, HEAD_A, HEAD_A), x_prompt.dtype)
    zeros_r = jnp.zeros((b_ctx, 2, N_HEADS_B, HEAD_B, HEAD_B), x_prompt.dtype)
    xp, xs = x_prompt, x_sample
    new_h, new_r = [], []
    for l in range(DEPTH):
        p = dict(norm_mix_w=norm_mix_w[l], w_in=w_in[l], lb=lb_all[l], hgrn_norm_w=hgrn_norm_w[l],
                 rwkv_conv=rwkv_conv[l], rwkv_w0=rwkv_w0[l], rwkv_w2=rwkv_w2[l], rwkv_a0=rwkv_a0[l],
                 rwkv_a2=rwkv_a2[l], rwkv_g2=rwkv_g2[l], rwkv_k_k=rwkv_k_k[l], rwkv_k_a=rwkv_k_a[l],
                 rwkv_r_k=rwkv_r_k[l], rwkv_ln_w=rwkv_ln_w[l], rwkv_ln_b=rwkv_ln_b[l], w_out=w_out[l],
                 norm_ffn_w=norm_ffn_w[l], ffn_w_gate=ffn_w_gate[l], ffn_w_up=ffn_w_up[l],
                 ffn_conv_b=ffn_conv_b[l], ffn_w_down=ffn_w_down[l])
        conv_l = ffn_conv[l]
        mod_ctx = jax.nn.silu(c_ctx)[None, :] @ ada_w[l] + ada_b[l]
        xp, s_h, s_r = block(xp, mod_ctx, zeros_h, zeros_r, lambda z: dwconv1d(z, conv_l[1]), p)
        new_h.append(s_h)
        new_r.append(s_r)
        mod_lat = jax.nn.silu(c) @ ada_w[l] + ada_b[l]
        xs, _, _ = block(xs, mod_lat, state_hgrn[:, l], state_rwkv[:, l], lambda z: dwconv2d_grid(z, conv_l), p)
    y_prompt = rmsnorm(xp, final_norm_w)
    y_sample = rmsnorm(xs, final_norm_w)
    new_state_hgrn = jnp.stack(new_h, axis=1)
    new_state_rwkv = jnp.stack(new_r, axis=1)
    return (y_prompt, y_sample, new_state_hgrn, new_state_rwkv)
```

```python
import functools

import numpy as np
import jax
import jax.numpy as jnp
from jax import lax
from jax.experimental import pallas as pl
from jax.experimental.pallas import tpu as pltpu

F32 = jnp.float32
BF16 = jnp.bfloat16

D_MODEL = 1024
GRID_W = 64
WIDTH_A = 512
HEAD_A = 128
N_HEADS_A = 4
WIDTH_B = 512
HEAD_B = 64
N_HEADS_B = 8
LORA_W = 32
LORA_A = 32
LORA_G = 96
D_FF = 2816
RMS_EPS = 1e-6
GN_EPS = 64e-5
DECAY_SCALE = 0.6065306597
P_A = 5 * WIDTH_A
P_B = 3 * WIDTH_B + 2 * LORA_W + LORA_A + LORA_G
LORA_COLS = 256
P_B_PAD = 3 * WIDTH_B + LORA_COLS

CHUNK = 64
LANES = 128
ROW_TILE = 256
VMEM_LIMIT = 56 * 1024 * 1024

NN = (((1,), (0,)), ((), ()))
NT = (((1,), (1,)), ((), ()))
TN = (((0,), (0,)), ((), ()))


def _dot(a, b, dims=NN):
    return lax.dot_general(a, b, dims, preferred_element_type=F32)


def _split(a):
    hi = a.astype(BF16)
    lo = (a - hi.astype(F32)).astype(BF16)
    return hi, lo


def _mm(a, b, dims=NN, passes=1):
    if passes == 1:
        return _dot(a.astype(BF16), b.astype(BF16), dims)
    ah, al = _split(a)
    bh, bl = _split(b)
    return _dot(ah, bh, dims) + (_dot(ah, bl, dims) + _dot(al, bh, dims))


def _mm_exact_lhs(m, x, dims=NN):
    xh, xl = _split(x)
    xm = (x - xh.astype(F32) - xl.astype(F32)).astype(BF16)
    return _dot(m, xh, dims) + (_dot(m, xl, dims) + _dot(m, xm, dims))


def _sigmoid(x):
    return 1.0 / (1.0 + jnp.exp(-x))


def _silu(x):
    return x * _sigmoid(x)


def _hgrn_level_consts(c, rev):
    t = np.arange(c)
    mats = [(t[None, :] <= t[:, None]).astype(np.float32)]
    masks = [np.eye(c, dtype=np.float32)]
    n = 1
    while n < c:
        blk, half = t // (2 * n), (t // n) % 2
        mid = blk * 2 * n + n
        m = np.zeros((c, c), np.float32)
        for row in range(c):
            if half[row] == 1:
                m[row, mid[row]:row + 1] = 1.0
            else:
                m[row, row + 1:mid[row]] = 1.0
        mats.append(m)
        masks.append(((blk[:, None] == blk[None, :]) & (half[:, None] == 1) & (half[None, :] == 0))
                     .astype(np.float32))
        n *= 2
    if rev:
        mats = [m[::-1, ::-1] for m in mats]
        masks = [m[::-1, ::-1] for m in masks]
    return np.concatenate(mats, 0), np.concatenate(masks, 0)


def _hgrn_consts(c):
    mf, kf = _hgrn_level_consts(c, False)
    mb, kb = _hgrn_level_consts(c, True)
    return np.stack([mf, mb]), np.stack([kf, kb])


def _rwkv_consts(c):
    t = np.arange(c)
    out = []
    for rev in (False, True):
        incl = (t[None, :] <= t[:, None]) if not rev else (t[None, :] >= t[:, None])
        strict = (t[None, :] < t[:, None]) if not rev else (t[None, :] > t[:, None])
        z = np.zeros((c, c), bool)
        cum = np.block([[incl, z], [z, z]])
        out.append(np.stack([cum, np.block([[strict, z], [z, strict]]), np.block([[incl, z], [z, incl]])]))
    return np.stack(out).astype(np.float32)


def _hgrn_chunk(q, k, v, g, st, m, masks, rev):
    c = q.shape[0]
    levels = m.shape[0] // c
    e_all = _mm_exact_lhs(m, g)
    b = e_all[:c]
    edge = b[0:1] if rev else b[c - 1:c]
    o = _mm(q * jnp.exp(b), st, NT)
    sc = masks[:c] * _mm(q, k, NT)
    for l in range(1, levels):
        e = jnp.exp(e_all[l * c:(l + 1) * c])
        sc = sc + masks[l * c:(l + 1) * c] * _mm(q * e, k * e, NT)
    o = o + _mm(sc, v)
    st = st * jnp.exp(edge) + _mm(v, k * jnp.exp(edge - b), TN)
    return o, st


def _stack_heads(x, lane_head):
    return jnp.concatenate([jnp.where(lane_head == 0, x, 0.0), jnp.where(lane_head == 1, x, 0.0)], axis=0)


def _rwkv_chunk(r, k, v, kk, beta, lw, s, consts, rev):
    c = r.shape[0]
    lane_head = lax.broadcasted_iota(jnp.int32, (c, LANES), 1) // HEAD_B
    cum, strict, incl = consts[0], consts[1], consts[2]
    g = _mm_exact_lhs(cum[:c, :c].astype(BF16), lw)
    edge = g[0:1] if rev else g[c - 1:c]
    e_in = jnp.exp(g)
    e_out = jnp.exp(-g)
    e_edge = jnp.exp(edge - g)
    st = lambda x: _stack_heads(x, lane_head)
    kg = st(kk * jnp.exp(g - lw))
    rg = st(r * e_in)
    bi = st(beta * e_out)
    ki = st(k * e_out)
    vs = st(v)
    a_all = _mm(jnp.concatenate([kg, rg], 0), jnp.concatenate([bi, ki], 0), NT, passes=3)
    n = strict * a_all[:2 * c, :2 * c]
    a_ak = strict * a_all[:2 * c, 2 * c:]
    a_rb = incl * a_all[2 * c:, :2 * c]
    a_rk = incl * a_all[2 * c:, 2 * c:]
    av = _mm(jnp.concatenate([a_ak, a_rk], 0), vs, passes=3)
    x = jnp.concatenate([kg, av[:2 * c]], axis=1)
    powers = [n]
    while (1 << len(powers)) < c:
        powers.append(_mm(powers[-1], powers[-1], passes=3))
    for p in reversed(powers[1:]):
        x = x + _mm(p, x, passes=3)
    x = x - _mm(n, x, passes=3)
    w1, x2 = x[:, :LANES], x[:, LANES:]
    hs = _mm(jnp.concatenate([w1, rg], 0), s, NT, passes=3)
    u = -(hs[:2 * c] + x2)
    y = hs[2 * c:] + av[2 * c:] + _mm(a_rb, u, passes=3)
    s = s * jnp.exp(edge) + _mm(u, st(beta * e_edge), TN, passes=3) + _mm(vs, st(k * e_edge), TN, passes=3)
    return y[:c] + y[c:], s


def _params(*semantics):
    return pltpu.CompilerParams(dimension_semantics=semantics, vmem_limit_bytes=VMEM_LIMIT)


def _full(shape):
    return pl.BlockSpec(shape, lambda *_: (0,) * len(shape))


def _modulated_norm(x, norm_w, scale, shift):
    y = x * lax.rsqrt(jnp.mean(x * x, axis=-1, keepdims=True) + RMS_EPS)
    return (y * norm_w) * (1.0 + scale) + shift


def _mod_kernel(c_ref, w_ref, b_ref, o_ref):
    o_ref[...] = _mm(_silu(c_ref[...]), w_ref[...], passes=3) + b_ref[...]


def _modulation(cvec, ada_w, ada_b):
    n = ada_w.shape[1]
    tn = n // 4
    return pl.pallas_call(
        _mod_kernel,
        grid=(n // tn,),
        in_specs=[_full(cvec.shape), pl.BlockSpec((D_MODEL, tn), lambda j: (0, j)), pl.BlockSpec((1, tn), lambda j: (0, j))],
        out_specs=pl.BlockSpec((cvec.shape[0], tn), lambda j: (0, j)),
        out_shape=jax.ShapeDtypeStruct((cvec.shape[0], n), F32),
        compiler_params=_params("arbitrary"),
        name="modulation",
    )(cvec, ada_w, ada_b)


def _in_proj_kernel(x_ref, xp_ref, xn_ref, mod_ref, nw_ref, wa_ref, wb_ref, cw_ref, ua_ref, ub_ref, *, seq_tiles):
    i = pl.program_id(0)
    mod = mod_ref[...]
    shift, scale = mod[:, 0:D_MODEL], mod[:, D_MODEL:2 * D_MODEL]
    nw = nw_ref[...]
    h = _modulated_norm(x_ref[...], nw, scale, shift).astype(BF16)
    ua_ref[...] = _dot(h, wa_ref[...])
    ub = _dot(h, wb_ref[...])
    halo = jnp.concatenate([xp_ref[...], xn_ref[...]], axis=0)
    hh = _modulated_norm(halo, nw, scale, shift).astype(BF16)
    ubh = _dot(hh, wb_ref[...])
    first = (i % seq_tiles) == 0
    last = (i % seq_tiles) == seq_tiles - 1
    prev_row = jnp.where(first, 0.0, ubh[7:8])
    next_row = jnp.where(last, 0.0, ubh[8:9])
    rows = lax.broadcasted_iota(jnp.int32, ub.shape, 0)
    n = ub.shape[0]
    below = jnp.where(rows == 0, prev_row, pltpu.roll(ub, 1, 0))
    above = jnp.where(rows == n - 1, next_row, pltpu.roll(ub, n - 1, 0))
    cw = cw_ref[...]
    ub_ref[...] = cw[0:1] * below + cw[1:2] * ub + cw[2:3] * above


def _in_proj(x, mod, norm_w, w_a, w_b, conv_w, seq_tiles, mod_of_tile):
    t = x.shape[0]
    nt = t // ROW_TILE
    sub = ROW_TILE // 8
    return pl.pallas_call(
        functools.partial(_in_proj_kernel, seq_tiles=seq_tiles),
        grid=(nt,),
        in_specs=[
            pl.BlockSpec((ROW_TILE, D_MODEL), lambda i: (i, 0)),
            pl.BlockSpec((8, D_MODEL), lambda i: (jnp.maximum(i * sub - 1, 0), 0)),
            pl.BlockSpec((8, D_MODEL), lambda i: (jnp.minimum((i + 1) * sub, nt * sub - 1), 0)),
            pl.BlockSpec((None, 1, 6 * D_MODEL), lambda i: (mod_of_tile(i), 0, 0)),
            _full((1, D_MODEL)), _full(w_a.shape), _full(w_b.shape), _full(conv_w.shape),
        ],
        out_specs=[pl.BlockSpec((ROW_TILE, P_A), lambda i: (i, 0)), pl.BlockSpec((ROW_TILE, P_B_PAD), lambda i: (i, 0))],
        out_shape=[jax.ShapeDtypeStruct((t, P_A), F32), jax.ShapeDtypeStruct((t, P_B_PAD), F32)],
        compiler_params=_params("parallel"),
        name="in_proj",
    )(x, x, x, mod, norm_w, w_a, w_b, conv_w)


def _hgrn_kernel(q_ref, i_ref, zf_ref, zb_ref, g_ref, lb_ref, nw_ref, s0_ref, m_ref, mask_ref,
                 o_ref, sout_ref, st_ref, *, nc):
    c = CHUNK
    p0, p1 = lb_ref[0], lb_ref[1]
    mx = jnp.maximum(p0, p1)
    e0, e1 = jnp.exp(p0 - mx), jnp.exp(p1 - mx)
    lb = e0 / (e0 + e1)
    nw = nw_ref[...]
    st_ref[...] = s0_ref[...]

    def rows_of(ci):
        return pl.ds(pl.multiple_of(ci * c, c), c)

    def direction(ci, d):
        rows = rows_of(ci)
        z = (zf_ref, zb_ref)[d][rows, :]
        lo = lb[d:d + 1]
        f = lo + (1.0 - lo) * _sigmoid(z)
        o, st = _hgrn_chunk(_silu(q_ref[rows, :]), 1.0 - f, i_ref[rows, :], jnp.log(f), st_ref[d],
                            m_ref[d], mask_ref[d], rev=(d == 1))
        st_ref[d] = st
        return o

    def finish(rows, o):
        o = o * lax.rsqrt(jnp.mean(o * o, axis=-1, keepdims=True) + RMS_EPS) * nw
        o_ref[rows, :] = o * _silu(g_ref[rows, :])

    def first_touch(ci, carry):
        cb = nc - 1 - ci
        o_ref[rows_of(ci), :] = direction(ci, 0)
        o_ref[rows_of(cb), :] = direction(cb, 1)
        return carry

    def second_touch(ci, carry):
        cb = nc - 1 - ci
        finish(rows_of(ci), o_ref[rows_of(ci), :] + direction(ci, 0))
        finish(rows_of(cb), o_ref[rows_of(cb), :] + direction(cb, 1))
        return carry

    lax.fori_loop(0, nc // 2, first_touch, 0)
    lax.fori_loop(nc // 2, nc, second_touch, 0)
    sout_ref[...] = st_ref[...]


def _hgrn_mixer(u_a, lb_raw, norm_w, s0t, seq_len):
    t = u_a.shape[0]
    nb = t // seq_len
    nc = seq_len // CHUNK
    assert nc % 2 == 0
    mats, masks = _hgrn_consts(CHUNK)
    mats = jnp.asarray(mats, BF16)
    masks = jnp.asarray(masks, F32)
    col = lambda part: pl.BlockSpec((seq_len, HEAD_A), lambda b, h: (b, part * N_HEADS_A + h))
    state_spec = pl.BlockSpec((None, 2, None, HEAD_A, HEAD_A), lambda b, h: (b, 0, h, 0, 0))
    return pl.pallas_call(
        functools.partial(_hgrn_kernel, nc=nc),
        grid=(nb, N_HEADS_A),
        in_specs=[col(0), col(1), col(2), col(3), col(4),
                  pl.BlockSpec((2, 2, HEAD_A), lambda b, h: (0, 0, h)),
                  pl.BlockSpec((1, HEAD_A), lambda b, h: (0, h)),
                  state_spec, _full(mats.shape), _full(masks.shape)],
        out_specs=[pl.BlockSpec((seq_len, HEAD_A), lambda b, h: (b, h)), state_spec],
        out_shape=[jax.ShapeDtypeStruct((t, WIDTH_A), F32), jax.ShapeDtypeStruct((nb, 2, N_HEADS_A, HEAD_A, HEAD_A), F32)],
        scratch_shapes=[pltpu.VMEM((2, HEAD_A, HEAD_A), F32)],
        compiler_params=_params("parallel", "parallel"),
        name="hgrn2_scan",
    )(u_a, u_a, u_a, u_a, u_a, lb_raw, norm_w, s0t, mats, masks)


def _mm_exact_rhs(x, m):
    xh, xl = _split(x)
    xm = (x - xh.astype(F32) - xl.astype(F32)).astype(BF16)
    return _dot(xh, m) + (_dot(xl, m) + _dot(xm, m))


def _rwkv_kernel(r_ref, k_ref, v_ref, lora_ref, w0_ref, a0_ref, kkw_ref, ka_ref, rk_ref, lnw_ref, lnb_ref,
                 wdec_ref, wa_ref, wg_ref, s0_ref, consts_ref, seg_ref,
                 o_ref, sout_ref, lwf_s, lwb_s, kk_s, beta_s, kmod_s, st_ref, *, nc):
    c = CHUNK
    seg = seg_ref[...]

    for d in range(2):
        st_ref[d] = jnp.zeros((LANES, LANES), F32)
        st_ref[d, 0:HEAD_B, 0:HEAD_B] = s0_ref[d, 0]
        st_ref[d, HEAD_B:LANES, HEAD_B:LANES] = s0_ref[d, 1]

    def rows_of(ci):
        return pl.ds(pl.multiple_of(ci * c, c), c)

    def prepare(ci, carry):
        rows = rows_of(ci)
        lora = lora_ref[rows, :]
        dec = _mm(jnp.tanh(lora), wdec_ref[...], passes=3)
        w0 = w0_ref[...]
        lwf_s[rows, :] = -DECAY_SCALE * _sigmoid(w0[0:1] + dec[:, :LANES])
        lwb_s[rows, :] = -DECAY_SCALE * _sigmoid(w0[1:2] + dec[:, LANES:])
        a = _sigmoid(a0_ref[...] + _mm(lora, wa_ref[...]))
        k = k_ref[rows, :]
        kk = k * kkw_ref[...]
        kk = kk * lax.rsqrt(_mm_exact_rhs(kk * kk, seg) + 1e-12)
        kk_s[rows, :] = kk
        beta_s[rows, :] = kk * a
        kmod_s[rows, :] = k * (1.0 + (a - 1.0) * ka_ref[...])
        return carry

    lax.fori_loop(0, nc, prepare, 0)

    def direction(ci, d):
        rows = rows_of(ci)
        lw = (lwf_s, lwb_s)[d][rows, :]
        y, s = _rwkv_chunk(r_ref[rows, :], kmod_s[rows, :], v_ref[rows, :], kk_s[rows, :], beta_s[rows, :], lw,
                           st_ref[d], consts_ref[d], rev=(d == 1))
        st_ref[d] = s
        return y

    def finish(rows, y):
        inv_n = 1.0 / HEAD_B
        mu = _mm_exact_rhs(y, seg) * inv_n
        dy = y - mu
        var = _mm_exact_rhs(dy * dy, seg) * inv_n
        yn = dy * lax.rsqrt(var + GN_EPS) * lnw_ref[...] + lnb_ref[...]
        v = v_ref[rows, :]
        bonus = _mm_exact_rhs(r_ref[rows, :] * kmod_s[rows, :] * rk_ref[...], seg) * v
        g = _mm(_sigmoid(lora_ref[rows, :]), wg_ref[...])
        o_ref[rows, :] = (yn + bonus) * g

    def first_touch(ci, carry):
        cb = nc - 1 - ci
        o_ref[rows_of(ci), :] = direction(ci, 0)
        o_ref[rows_of(cb), :] = direction(cb, 1)
        return carry

    def second_touch(ci, carry):
        cb = nc - 1 - ci
        finish(rows_of(ci), o_ref[rows_of(ci), :] + direction(ci, 0))
        finish(rows_of(cb), o_ref[rows_of(cb), :] + direction(cb, 1))
        return carry

    lax.fori_loop(0, nc // 2, first_touch, 0)
    lax.fori_loop(nc // 2, nc, second_touch, 0)
    for d in range(2):
        sout_ref[d, 0] = st_ref[d, 0:HEAD_B, 0:HEAD_B]
        sout_ref[d, 1] = st_ref[d, HEAD_B:LANES, HEAD_B:LANES]


def _rwkv_mixer(u_b, p, s0, seq_len):
    t = u_b.shape[0]
    nb = t // seq_len
    nc = seq_len // CHUNK
    assert nc % 2 == 0
    pairs = N_HEADS_B // 2
    consts = jnp.asarray(_rwkv_consts(CHUNK), F32)
    lane = np.arange(LANES) // HEAD_B
    seg = jnp.asarray(lane[:, None] == lane[None, :], BF16)
    col = lambda part: pl.BlockSpec((seq_len, LANES), lambda b, h: (b, part * pairs + h))
    vec = lambda rows: pl.BlockSpec((rows, LANES), lambda b, h: (0, h))
    state_spec = pl.BlockSpec((None, 2, 2, HEAD_B, HEAD_B), lambda b, h: (b, 0, h, 0, 0))
    scr = lambda: pltpu.VMEM((seq_len, LANES), F32)
    return pl.pallas_call(
        functools.partial(_rwkv_kernel, nc=nc),
        grid=(nb, pairs),
        in_specs=[col(0), col(1), col(2),
                  pl.BlockSpec((seq_len, LORA_COLS), lambda b, h: (b, 3 * WIDTH_B // LORA_COLS)),
                  vec(2), vec(1), vec(1), vec(1), vec(1), vec(1), vec(1),
                  pl.BlockSpec((None, LORA_COLS, 2 * LANES), lambda b, h: (h, 0, 0)),
                  pl.BlockSpec((LORA_COLS, LANES), lambda b, h: (0, h)),
                  pl.BlockSpec((LORA_COLS, LANES), lambda b, h: (0, h)),
                  state_spec, _full(consts.shape), _full(seg.shape)],
        out_specs=[pl.BlockSpec((seq_len, LANES), lambda b, h: (b, h)), state_spec],
        out_shape=[jax.ShapeDtypeStruct((t, WIDTH_B), F32), jax.ShapeDtypeStruct((nb, 2, N_HEADS_B, HEAD_B, HEAD_B), F32)],
        scratch_shapes=[scr(), scr(), scr(), scr(), scr(), pltpu.VMEM((2, LANES, LANES), F32)],
        compiler_params=_params("parallel", "parallel"),
        name="rwkv7_scan",
    )(u_b, u_b, u_b, u_b, p["w0"], p["a0"], p["k_k"], p["k_a"], p["r_k"], p["ln_w"], p["ln_b"],
      p["w_dec"], p["w_a"], p["w_g"], s0, consts, seg)


def _mix_out_kernel(oa_ref, ob_ref, x_ref, mod_ref, woa_ref, wob_ref, nw_ref, wg_ref, wu_ref, x1_ref, g_ref, u_ref):
    mod = mod_ref[...]
    gate1 = mod[:, 2 * D_MODEL:3 * D_MODEL]
    shift2, scale2 = mod[:, 3 * D_MODEL:4 * D_MODEL], mod[:, 4 * D_MODEL:5 * D_MODEL]
    mix = _dot(oa_ref[...].astype(BF16), woa_ref[...]) + _dot(ob_ref[...].astype(BF16), wob_ref[...])
    x1 = x_ref[...] + gate1 * mix
    x1_ref[...] = x1
    h = _modulated_norm(x1, nw_ref[...], scale2, shift2).astype(BF16)
    g_ref[...] = _dot(h, wg_ref[...])
    u_ref[...] = _dot(h, wu_ref[...])


def _mix_out(o_a, o_b, x, mod, w_out_a, w_out_b, norm_w, w_gate, w_up, mod_of_tile):
    t = x.shape[0]
    row = lambda n: pl.BlockSpec((ROW_TILE, n), lambda i: (i, 0))
    return pl.pallas_call(
        _mix_out_kernel,
        grid=(t // ROW_TILE,),
        in_specs=[row(WIDTH_A), row(WIDTH_B), row(D_MODEL),
                  pl.BlockSpec((None, 1, 6 * D_MODEL), lambda i: (mod_of_tile(i), 0, 0)),
                  _full(w_out_a.shape), _full(w_out_b.shape), _full((1, D_MODEL)), _full(w_gate.shape), _full(w_up.shape)],
        out_specs=[row(D_MODEL), row(D_FF), row(D_FF)],
        out_shape=[jax.ShapeDtypeStruct((t, D_MODEL), F32), jax.ShapeDtypeStruct((t, D_FF), F32),
                   jax.ShapeDtypeStruct((t, D_FF), F32)],
        compiler_params=_params("parallel"),
        name="mix_out_ffn_in",
    )(o_a, o_b, x, mod, w_out_a, w_out_b, norm_w, w_gate, w_up)


def _gelu_tanh(x):
    return 0.5 * x * (1.0 + jnp.tanh(0.7978845608028654 * (x + 0.044715 * (x * x * x))))


def _ffn_out_kernel(g_ref, ga_ref, gb_ref, u_ref, x1_ref, mod_ref, cw_ref, cb_ref, wd_ref, fw_ref, y_ref,
                    *, seq_tiles, grid_conv):
    i = pl.program_id(0)
    cw = cw_ref[...]
    g = g_ref[...]
    n = g.shape[0]
    if grid_conv:
        top = (i % seq_tiles) == 0
        bottom = (i % seq_tiles) == seq_tiles - 1
        ext = jnp.concatenate([jnp.where(top, 0.0, ga_ref[...]), g, jnp.where(bottom, 0.0, gb_ref[...])], axis=0)
        width = GRID_W
        taps = (0, 1, 2)
    else:
        ext = g
        width = n
        taps = (1,)
    ne = ext.shape[0]
    col = lax.broadcasted_iota(jnp.int32, ext.shape, 0) % width
    left = jnp.where(col == 0, 0.0, pltpu.roll(ext, 1, 0))
    right = jnp.where(col == width - 1, 0.0, pltpu.roll(ext, ne - 1, 0))
    acc = None
    for dr in taps:
        off = (GRID_W * dr) if grid_conv else 0
        for dc, src in enumerate((left, ext, right)):
            term = cw[3 * dr + dc:3 * dr + dc + 1] * src[off:off + n]
            acc = term if acc is None else acc + term
    gt = acc + cb_ref[...]
    act = (_gelu_tanh(gt) * u_ref[...]).astype(BF16)
    mod = mod_ref[...]
    gate2 = mod[:, 5 * D_MODEL:6 * D_MODEL]
    x2 = x1_ref[...] + gate2 * _dot(act, wd_ref[...])
    y_ref[...] = x2 * lax.rsqrt(jnp.mean(x2 * x2, axis=-1, keepdims=True) + RMS_EPS) * fw_ref[...]


def _ffn_out(gpre, up, x1, mod, conv_w, conv_b, w_down, final_w, seq_tiles, grid_conv, mod_of_tile):
    t = x1.shape[0]
    nt = t // ROW_TILE
    per = ROW_TILE // GRID_W
    row = lambda n: pl.BlockSpec((ROW_TILE, n), lambda i: (i, 0))
    return pl.pallas_call(
        functools.partial(_ffn_out_kernel, seq_tiles=seq_tiles, grid_conv=grid_conv),
        grid=(nt,),
        in_specs=[row(D_FF),
                  pl.BlockSpec((GRID_W, D_FF), lambda i: (jnp.maximum(i * per - 1, 0), 0)),
                  pl.BlockSpec((GRID_W, D_FF), lambda i: (jnp.minimum((i + 1) * per, nt * per - 1), 0)),
                  row(D_FF), row(D_MODEL),
                  pl.BlockSpec((None, 1, 6 * D_MODEL), lambda i: (mod_of_tile(i), 0, 0)),
                  _full(conv_w.shape), _full((1, D_FF)), _full(w_down.shape), _full((1, D_MODEL))],
        out_specs=row(D_MODEL),
        out_shape=jax.ShapeDtypeStruct((t, D_MODEL), F32),
        compiler_params=_params("parallel"),
        name="ffn_out",
    )(gpre, gpre, gpre, up, x1, mod, conv_w, conv_b, w_down, final_w)


def _block(x, seq_len, mod, mod_of_tile, s_hgrn_t, s_rwkv, grid_conv, w):
    seq_tiles = seq_len // ROW_TILE
    u_a, u_b = _in_proj(x, mod, w["norm_mix_w"], w["w_in_a"], w["w_in_b"], w["rwkv_conv"], seq_tiles, mod_of_tile)
    o_a, s_h = _hgrn_mixer(u_a, w["hgrn_lb"], w["hgrn_norm_w"], s_hgrn_t, seq_len)
    o_b, s_r = _rwkv_mixer(u_b, w["rwkv"], s_rwkv, seq_len)
    x1, gpre, up = _mix_out(o_a, o_b, x, mod, w["w_out_a"], w["w_out_b"], w["norm_ffn_w"], w["ffn_w_gate"],
                            w["ffn_w_up"], mod_of_tile)
    y = _ffn_out(gpre, up, x1, mod, w["ffn_conv"], w["ffn_conv_b"], w["ffn_w_down"], w["final_norm_w"],
                 seq_tiles, grid_conv, mod_of_tile)
    return y, s_h, s_r


def _place_rows(w, start, total):
    return jnp.zeros((total, w.shape[1]), w.dtype).at[start:start + w.shape[0]].set(w)


def kernel(x_prompt, x_sample, state_hgrn, state_rwkv, c, c_ctx, ada_w, ada_b, norm_mix_w, w_in, hgrn_lb, hgrn_norm_w, rwkv_conv, rwkv_w0, rwkv_w2, rwkv_a0, rwkv_a2, rwkv_g2, rwkv_k_k, rwkv_k_a, rwkv_r_k, rwkv_ln_w, rwkv_ln_b, w_out, norm_ffn_w, ffn_w_gate, ffn_w_up, ffn_conv, ffn_conv_b, ffn_w_down, final_norm_w):
    assert w_in.shape[0] == 1, "one trunk layer"
    b_ctx, t_ctx, _ = x_prompt.shape
    b_lat, t_lat, _ = x_sample.shape
    row = lambda v: v.reshape(1, -1)
    pad_cols = P_B_PAD - P_B
    w_dec = jnp.concatenate([_place_rows(rwkv_w2[0, 0], 0, LORA_COLS), _place_rows(rwkv_w2[0, 1], LORA_W, LORA_COLS)], axis=1)
    pairs = N_HEADS_B // 2
    w_dec = w_dec.reshape(LORA_COLS, 2, pairs, LANES).transpose(2, 0, 1, 3).reshape(pairs, LORA_COLS, 2 * LANES)
    weights = dict(
        norm_mix_w=row(norm_mix_w[0]),
        w_in_a=w_in[0, :, :P_A].astype(BF16),
        w_in_b=jnp.pad(w_in[0, :, P_A:], ((0, 0), (0, pad_cols))).astype(BF16),
        rwkv_conv=jnp.pad(rwkv_conv[0], ((0, 0), (0, pad_cols))),
        hgrn_lb=hgrn_lb,
        hgrn_norm_w=row(hgrn_norm_w[0]),
        rwkv=dict(w0=rwkv_w0[0], a0=row(rwkv_a0[0]), k_k=row(rwkv_k_k[0]), k_a=row(rwkv_k_a[0]), r_k=row(rwkv_r_k[0]),
                  ln_w=row(rwkv_ln_w[0]), ln_b=row(rwkv_ln_b[0]), w_dec=w_dec,
                  w_a=_place_rows(rwkv_a2[0], 2 * LORA_W, LORA_COLS),
                  w_g=_place_rows(rwkv_g2[0], 2 * LORA_W + LORA_A, LORA_COLS)),
        w_out_a=w_out[0, :WIDTH_A].astype(BF16),
        w_out_b=w_out[0, WIDTH_A:].astype(BF16),
        norm_ffn_w=row(norm_ffn_w[0]),
        ffn_w_gate=ffn_w_gate[0].astype(BF16),
        ffn_w_up=ffn_w_up[0].astype(BF16),
        ffn_conv=ffn_conv[0].reshape(9, D_FF),
        ffn_conv_b=row(ffn_conv_b[0]),
        ffn_w_down=ffn_w_down[0].astype(BF16),
        final_norm_w=row(final_norm_w),
    )
    cvec = jnp.concatenate([c_ctx[None, :], c, jnp.zeros((8 - 1 - b_lat, D_MODEL), F32)], axis=0)
    mod = _modulation(cvec, ada_w[0], row(ada_b[0])).reshape(8, 1, 6 * D_MODEL)

    zeros_h = jnp.zeros((b_ctx, 2, N_HEADS_A, HEAD_A, HEAD_A), F32)
    zeros_r = jnp.zeros((b_ctx, 2, N_HEADS_B, HEAD_B, HEAD_B), F32)
    yp, s_h, s_r = _block(x_prompt.reshape(b_ctx * t_ctx, D_MODEL), t_ctx, mod, lambda i: 0, zeros_h, zeros_r, False, weights)
    lat_tiles = t_lat // ROW_TILE
    ys, _, _ = _block(x_sample.reshape(b_lat * t_lat, D_MODEL), t_lat, mod, lambda i: 1 + i // lat_tiles,
                      jnp.swapaxes(state_hgrn[:, 0], -1, -2), state_rwkv[:, 0], True, weights)
    y_prompt = yp.reshape(b_ctx, t_ctx, D_MODEL)
    y_sample = ys.reshape(b_lat, t_lat, D_MODEL)
    new_state_hgrn = jnp.swapaxes(s_h, -1, -2)[:, None]
    new_state_rwkv = s_r[:, None]
    return (y_prompt, y_sample, new_state_hgrn, new_state_rwkv)
```

```python
import functools

import numpy as np
import jax
import jax.numpy as jnp
from jax import lax
from jax.experimental import pallas as pl
from jax.experimental.pallas import tpu as pltpu

F32 = jnp.float32
BF16 = jnp.bfloat16

D_MODEL = 1024
GRID_W = 64
WIDTH_A = 512
HEAD_A = 128
N_HEADS_A = 4
WIDTH_B = 512
HEAD_B = 64
N_HEADS_B = 8
LORA_W = 32
LORA_A = 32
LORA_G = 96
D_FF = 2816
RMS_EPS = 1e-6
GN_EPS = 64e-5
DECAY_SCALE = 0.6065306597
P_A = 5 * WIDTH_A
P_B = 3 * WIDTH_B + 2 * LORA_W + LORA_A + LORA_G
LORA_COLS = 256
P_B_PAD = 3 * WIDTH_B + LORA_COLS

CHUNK = 64
LANES = 128
ROW_TILE = 256
VMEM_LIMIT = 56 * 1024 * 1024

NN = (((1,), (0,)), ((), ()))
NT = (((1,), (1,)), ((), ()))
TN = (((0,), (0,)), ((), ()))


def _dot(a, b, dims=NN):
    return lax.dot_general(a, b, dims, preferred_element_type=F32)


def _split(a):
    hi = a.astype(BF16)
    lo = (a - hi.astype(F32)).astype(BF16)
    return hi, lo


def _mm(a, b, dims=NN, passes=1):
    if passes == 1:
        return _dot(a.astype(BF16), b.astype(BF16), dims)
    ah, al = _split(a)
    bh, bl = _split(b)
    return _dot(ah, bh, dims) + (_dot(ah, bl, dims) + _dot(al, bh, dims))


def _mm_exact_lhs(m, x, dims=NN):
    xh, xl = _split(x)
    xm = (x - xh.astype(F32) - xl.astype(F32)).astype(BF16)
    return _dot(m, xh, dims) + (_dot(m, xl, dims) + _dot(m, xm, dims))


def _sigmoid(x):
    return 1.0 / (1.0 + jnp.exp(-x))


def _silu(x):
    return x * _sigmoid(x)


def _hgrn_level_consts(c, rev):
    t = np.arange(c)
    mats = [(t[None, :] <= t[:, None]).astype(np.float32)]
    masks = [np.eye(c, dtype=np.float32)]
    n = 1
    while n < c:
        blk, half = t // (2 * n), (t // n) % 2
        mid = blk * 2 * n + n
        m = np.zeros((c, c), np.float32)
        for row in range(c):
            if half[row] == 1:
                m[row, mid[row]:row + 1] = 1.0
            else:
                m[row, row + 1:mid[row]] = 1.0
        mats.append(m)
        masks.append(((blk[:, None] == blk[None, :]) & (half[:, None] == 1) & (half[None, :] == 0))
                     .astype(np.float32))
        n *= 2
    if rev:
        mats = [m[::-1, ::-1] for m in mats]
        masks = [m[::-1, ::-1] for m in masks]
    return np.concatenate(mats, 0), np.concatenate(masks, 0)


def _hgrn_consts(c):
    mf, kf = _hgrn_level_consts(c, False)
    mb, kb = _hgrn_level_consts(c, True)
    return np.stack([mf, mb]), np.stack([kf, kb])


def _rwkv_consts(c):
    t = np.arange(c)
    out = []
    for rev in (False, True):
        incl = (t[None, :] <= t[:, None]) if not rev else (t[None, :] >= t[:, None])
        strict = (t[None, :] < t[:, None]) if not rev else (t[None, :] > t[:, None])
        z = np.zeros((c, c), bool)
        cum = np.block([[incl, z], [z, z]])
        out.append(np.stack([cum, np.block([[strict, z], [z, strict]]), np.block([[incl, z], [z, incl]])]))
    return np.stack(out).astype(np.float32)


def _each(fn, *cols):
    return [fn(*args) for args in zip(*cols)]


def _hgrn_chunks_local(units):
    q, k, v, g, m, masks, rev = (list(col) for col in zip(*units))
    c = q[0].shape[0]
    levels = m[0].shape[0] // c
    e_all = _each(_mm_exact_lhs, m, g)
    sc = _each(lambda mk, a, b: mk[:c] * _mm(a, b, NT), masks, q, k)
    for l in range(1, levels):
        e = _each(lambda x: jnp.exp(x[l * c:(l + 1) * c]), e_all)
        sc = _each(lambda s, mk, a, b, ee: s + mk[l * c:(l + 1) * c] * _mm(a * ee, b * ee, NT), sc, masks, q, k, e)
    intra = _each(_mm, sc, v)
    out = []
    for x, rv, qq, kk, ii in zip(e_all, rev, q, k, intra):
        b = x[:c]
        edge = b[0:1] if rv else b[c - 1:c]
        out.append((ii, qq * jnp.exp(b), kk * jnp.exp(edge - b), jnp.exp(edge)))
    return out


def _hgrn_chunks_state(states, parts, v):
    intra, qb, kdec, decay = (list(col) for col in zip(*parts))
    o = _each(lambda i, a, s: i + _mm(a, s, NT), intra, qb, states)
    st = _each(lambda s, d, vv, kd: s * d + _mm(vv, kd, TN), states, decay, v, kdec)
    return o, st


def _stack_heads(x, lane_head):
    return jnp.concatenate([jnp.where(lane_head == 0, x, 0.0), jnp.where(lane_head == 1, x, 0.0)], axis=0)


RWKV_PASSES_SCORE = 3
RWKV_PASSES_SOLVE = 1
RWKV_PASSES_STATE = 3
RWKV_LOCAL_PARTS = 7
RWKV_GROUP = 4
HGRN_GROUP = 4


def _rwkv_chunks_local(units):
    c = units[0][0].shape[0]
    lane_head = lax.broadcasted_iota(jnp.int32, (c, LANES), 1) // HEAD_B
    st = lambda x: _stack_heads(x, lane_head)
    r, k, v, kk, beta, lw, consts, rev = (list(col) for col in zip(*units))
    strict = [cs[1] for cs in consts]
    incl = [cs[2] for cs in consts]
    g = _each(lambda cs, x: _mm_exact_lhs(cs[0][:c, :c].astype(BF16), x), consts, lw)
    edge = _each(lambda x, rv: x[0:1] if rv else x[c - 1:c], g, rev)
    kg = _each(lambda x, gg, l: st(x * jnp.exp(gg - l)), kk, g, lw)
    rg = _each(lambda x, gg: st(x * jnp.exp(gg)), r, g)
    e_out = _each(lambda gg: jnp.exp(-gg), g)
    bi = _each(lambda x, e: st(x * e), beta, e_out)
    ki = _each(lambda x, e: st(x * e), k, e_out)
    vs = _each(st, v)
    a_all = _each(lambda a, b, cc, d: _mm(jnp.concatenate([a, b], 0), jnp.concatenate([cc, d], 0), NT, RWKV_PASSES_SCORE),
                  kg, rg, bi, ki)
    n = _each(lambda m, a: m * a[:2 * c, :2 * c], strict, a_all)
    a_ak = _each(lambda m, a: m * a[:2 * c, 2 * c:], strict, a_all)
    a_rb = _each(lambda m, a: m * a[2 * c:, :2 * c], incl, a_all)
    a_rk = _each(lambda m, a: m * a[2 * c:, 2 * c:], incl, a_all)
    av = _each(lambda a, b, x: _mm(jnp.concatenate([a, b], 0), x, passes=RWKV_PASSES_SCORE), a_ak, a_rk, vs)
    x = _each(lambda a, b: jnp.concatenate([a, b[:2 * c]], axis=1), kg, av)
    powers = [n]
    while (1 << len(powers)) < c:
        powers.append(_each(lambda p: _mm(p, p, passes=RWKV_PASSES_SOLVE), powers[-1]))
    for p in reversed(powers[1:]):
        x = _each(lambda pp, xx: xx + _mm(pp, xx, passes=RWKV_PASSES_SOLVE), p, x)
    x = _each(lambda pp, xx: xx - _mm(pp, xx, passes=RWKV_PASSES_SOLVE), n, x)
    e_edge = _each(lambda ed, gg: jnp.exp(ed - gg), edge, g)
    kv = _each(lambda a, b, e: _mm(a, st(b * e), TN, RWKV_PASSES_STATE), vs, k, e_edge)
    bgc = _each(lambda b, e: st(b * e), beta, e_edge)
    return [((xx[:, :LANES], rr, xx[:, LANES:], ab, a[2 * c:], bg, kvv), jnp.exp(ed))
            for xx, rr, ab, a, bg, kvv, ed in zip(x, rg, a_rb, av, bgc, kv, edge)]


def _rwkv_chunks_state(states, parts, decays):
    w1, rg, x2, a_rb, y0, bgc, kv = (list(col) for col in zip(*parts))
    c2 = w1[0].shape[0]
    hs = _each(lambda a, b, s: _mm(jnp.concatenate([a, b], 0), s, NT, RWKV_PASSES_STATE), w1, rg, states)
    u = _each(lambda h, x: -(h[:c2] + x), hs, x2)
    y = _each(lambda h, y_, a, uu: h[c2:] + y_ + _mm(a, uu, passes=RWKV_PASSES_STATE), hs, y0, a_rb, u)
    s = _each(lambda s_, d, uu, b, kv_: s_ * d + _mm(uu, b, TN, RWKV_PASSES_STATE) + kv_, states, decays, u, bgc, kv)
    return [yy[:c2 // 2] + yy[c2 // 2:] for yy in y], s


def _params(*semantics):
    return pltpu.CompilerParams(dimension_semantics=semantics, vmem_limit_bytes=VMEM_LIMIT)


def _full(shape):
    return pl.BlockSpec(shape, lambda *_: (0,) * len(shape))


def _modulated_norm(x, norm_w, scale, shift):
    y = x * lax.rsqrt(jnp.mean(x * x, axis=-1, keepdims=True) + RMS_EPS)
    return (y * norm_w) * (1.0 + scale) + shift


def _mod_kernel(c_ref, w_ref, b_ref, o_ref):
    o_ref[...] = _mm(_silu(c_ref[...]), w_ref[...], passes=3) + b_ref[...]


def _modulation(cvec, ada_w, ada_b):
    n = ada_w.shape[1]
    tn = n // 4
    return pl.pallas_call(
        _mod_kernel,
        grid=(n // tn,),
        in_specs=[_full(cvec.shape), pl.BlockSpec((D_MODEL, tn), lambda j: (0, j)), pl.BlockSpec((1, tn), lambda j: (0, j))],
        out_specs=pl.BlockSpec((cvec.shape[0], tn), lambda j: (0, j)),
        out_shape=jax.ShapeDtypeStruct((cvec.shape[0], n), F32),
        compiler_params=_params("arbitrary"),
        name="modulation",
    )(cvec, ada_w, ada_b)


def _in_proj_kernel(x_ref, xp_ref, xn_ref, mod_ref, nw_ref, wa_ref, wb_ref, cw_ref, ua_ref, ub_ref, *, seq_tiles):
    i = pl.program_id(0)
    mod = mod_ref[...]
    shift, scale = mod[:, 0:D_MODEL], mod[:, D_MODEL:2 * D_MODEL]
    nw = nw_ref[...]
    h = _modulated_norm(x_ref[...], nw, scale, shift).astype(BF16)
    ua_ref[...] = _dot(h, wa_ref[...])
    ub = _dot(h, wb_ref[...])
    halo = jnp.concatenate([xp_ref[...], xn_ref[...]], axis=0)
    hh = _modulated_norm(halo, nw, scale, shift).astype(BF16)
    ubh = _dot(hh, wb_ref[...])
    first = (i % seq_tiles) == 0
    last = (i % seq_tiles) == seq_tiles - 1
    prev_row = jnp.where(first, 0.0, ubh[7:8])
    next_row = jnp.where(last, 0.0, ubh[8:9])
    rows = lax.broadcasted_iota(jnp.int32, ub.shape, 0)
    n = ub.shape[0]
    below = jnp.where(rows == 0, prev_row, pltpu.roll(ub, 1, 0))
    above = jnp.where(rows == n - 1, next_row, pltpu.roll(ub, n - 1, 0))
    cw = cw_ref[...]
    ub_ref[...] = cw[0:1] * below + cw[1:2] * ub + cw[2:3] * above


def _in_proj(x, mod, norm_w, w_a, w_b, conv_w, seq_tiles, mod_of_tile):
    t = x.shape[0]
    nt = t // ROW_TILE
    sub = ROW_TILE // 8
    return pl.pallas_call(
        functools.partial(_in_proj_kernel, seq_tiles=seq_tiles),
        grid=(nt,),
        in_specs=[
            pl.BlockSpec((ROW_TILE, D_MODEL), lambda i: (i, 0)),
            pl.BlockSpec((8, D_MODEL), lambda i: (jnp.maximum(i * sub - 1, 0), 0)),
            pl.BlockSpec((8, D_MODEL), lambda i: (jnp.minimum((i + 1) * sub, nt * sub - 1), 0)),
            pl.BlockSpec((None, 1, 6 * D_MODEL), lambda i: (mod_of_tile(i), 0, 0)),
            _full((1, D_MODEL)), _full(w_a.shape), _full(w_b.shape), _full(conv_w.shape),
        ],
        out_specs=[pl.BlockSpec((ROW_TILE, P_A), lambda i: (i, 0)), pl.BlockSpec((ROW_TILE, P_B_PAD), lambda i: (i, 0))],
        out_shape=[jax.ShapeDtypeStruct((t, P_A), F32), jax.ShapeDtypeStruct((t, P_B_PAD), F32)],
        compiler_params=_params("parallel"),
        name="in_proj",
    )(x, x, x, mod, norm_w, w_a, w_b, conv_w)


def _hgrn_kernel(q_ref, i_ref, zf_ref, zb_ref, g_ref, lb_ref, nw_ref, s0_ref, m_ref, mask_ref,
                 o_ref, sout_ref, acc_s, loc_s, dec_s, st_ref, *, nc, group):
    c = CHUNK
    p0, p1 = lb_ref[0], lb_ref[1]
    mx = jnp.maximum(p0, p1)
    e0, e1 = jnp.exp(p0 - mx), jnp.exp(p1 - mx)
    lb = e0 / (e0 + e1)
    st_ref[...] = s0_ref[...]

    def rows_of(ci):
        return pl.ds(pl.multiple_of(ci * c, c), c)

    def chunk_of(gi, j, d):
        cf = gi * group + j
        return cf if d == 0 else nc - 1 - cf

    def scan_group(gi, carry):
        units = []
        for j in range(group):
            for d in range(2):
                rows = rows_of(chunk_of(gi, j, d))
                lo = lb[d:d + 1]
                f = lo + (1.0 - lo) * _sigmoid((zf_ref, zb_ref)[d][rows, :])
                units.append((_silu(q_ref[rows, :]), 1.0 - f, i_ref[rows, :], jnp.log(f), m_ref[d], mask_ref[d], d == 1))
        for u, (intra, qb, kdec, decay) in enumerate(_hgrn_chunks_local(units)):
            j, d = divmod(u, 2)
            loc_s[d, j, 0] = intra
            loc_s[d, j, 1] = qb
            loc_s[d, j, 2] = kdec
            dec_s[d, j] = decay
        st = [st_ref[0], st_ref[1]]
        for j in range(group):
            rows = [rows_of(chunk_of(gi, j, d)) for d in range(2)]
            parts = [(loc_s[d, j, 0], loc_s[d, j, 1], loc_s[d, j, 2], dec_s[d, j]) for d in range(2)]
            o, st = _hgrn_chunks_state(st, parts, [i_ref[rows[d], :] for d in range(2)])
            for d in range(2):
                acc_s[rows[d], :] += o[d]
        st_ref[0], st_ref[1] = st
        return carry

    acc_s[...] = jnp.zeros(acc_s.shape, F32)
    lax.fori_loop(0, nc // group, scan_group, 0)

    def finish(ci, carry):
        rows = rows_of(ci)
        o = acc_s[rows, :]
        o = o * lax.rsqrt(jnp.mean(o * o, axis=-1, keepdims=True) + RMS_EPS) * nw_ref[...]
        o_ref[rows, :] = o * _silu(g_ref[rows, :])
        return carry

    lax.fori_loop(0, nc, finish, 0)
    sout_ref[...] = st_ref[...]


def _hgrn_mixer(u_a, lb_raw, norm_w, s0t, seq_len):
    t = u_a.shape[0]
    nb = t // seq_len
    nc = seq_len // CHUNK
    group = min(HGRN_GROUP, nc)
    assert nc % group == 0
    mats, masks = _hgrn_consts(CHUNK)
    mats = jnp.asarray(mats, BF16)
    masks = jnp.asarray(masks, F32)
    col = lambda part: pl.BlockSpec((seq_len, HEAD_A), lambda b, h: (b, part * N_HEADS_A + h))
    state_spec = pl.BlockSpec((None, 2, None, HEAD_A, HEAD_A), lambda b, h: (b, 0, h, 0, 0))
    return pl.pallas_call(
        functools.partial(_hgrn_kernel, nc=nc, group=group),
        grid=(nb, N_HEADS_A),
        in_specs=[col(0), col(1), col(2), col(3), col(4),
                  pl.BlockSpec((2, 2, HEAD_A), lambda b, h: (0, 0, h)),
                  pl.BlockSpec((1, HEAD_A), lambda b, h: (0, h)),
                  state_spec, _full(mats.shape), _full(masks.shape)],
        out_specs=[pl.BlockSpec((seq_len, HEAD_A), lambda b, h: (b, h)), state_spec],
        out_shape=[jax.ShapeDtypeStruct((t, WIDTH_A), F32), jax.ShapeDtypeStruct((nb, 2, N_HEADS_A, HEAD_A, HEAD_A), F32)],
        scratch_shapes=[pltpu.VMEM((seq_len, HEAD_A), F32),
                        pltpu.VMEM((2, group, 3, CHUNK, HEAD_A), F32),
                        pltpu.VMEM((2, group, 1, HEAD_A), F32),
                        pltpu.VMEM((2, HEAD_A, HEAD_A), F32)],
        compiler_params=_params("parallel", "parallel"),
        name="hgrn2_scan",
    )(u_a, u_a, u_a, u_a, u_a, lb_raw, norm_w, s0t, mats, masks)


def _mm_exact_rhs(x, m):
    xh, xl = _split(x)
    xm = (x - xh.astype(F32) - xl.astype(F32)).astype(BF16)
    return _dot(xh, m) + (_dot(xl, m) + _dot(xm, m))


def _rwkv_kernel(r_ref, k_ref, v_ref, lora_ref, w0_ref, a0_ref, kkw_ref, ka_ref, rk_ref, lnw_ref, lnb_ref,
                 wdec_ref, wa_ref, wg_ref, s0_ref, consts_ref, seg_ref,
                 o_ref, sout_ref, lwf_s, lwb_s, kk_s, beta_s, kmod_s, y_s, loc_s, dec_s, st_ref, *, nc, group):
    c = CHUNK
    seg = seg_ref[...]

    for d in range(2):
        st_ref[d] = jnp.zeros((LANES, LANES), F32)
        st_ref[d, 0:HEAD_B, 0:HEAD_B] = s0_ref[d, 0]
        st_ref[d, HEAD_B:LANES, HEAD_B:LANES] = s0_ref[d, 1]

    def rows_of(ci):
        return pl.ds(pl.multiple_of(ci * c, c), c)

    def prepare(ci, carry):
        rows = rows_of(ci)
        lora = lora_ref[rows, :]
        dec = _mm(jnp.tanh(lora), wdec_ref[...], passes=3)
        w0 = w0_ref[...]
        lwf_s[rows, :] = -DECAY_SCALE * _sigmoid(w0[0:1] + dec[:, :LANES])
        lwb_s[rows, :] = -DECAY_SCALE * _sigmoid(w0[1:2] + dec[:, LANES:])
        a = _sigmoid(a0_ref[...] + _mm(lora, wa_ref[...]))
        k = k_ref[rows, :]
        kk = k * kkw_ref[...]
        kk = kk * lax.rsqrt(_mm_exact_rhs(kk * kk, seg) + 1e-12)
        kk_s[rows, :] = kk
        beta_s[rows, :] = kk * a
        kmod_s[rows, :] = k * (1.0 + (a - 1.0) * ka_ref[...])
        return carry

    lax.fori_loop(0, nc, prepare, 0)

    def chunk_of(gi, j, d):
        cf = gi * group + j
        return cf if d == 0 else nc - 1 - cf

    def scan_group(gi, carry):
        units = []
        for j in range(group):
            for d in range(2):
                rows = rows_of(chunk_of(gi, j, d))
                units.append((r_ref[rows, :], kmod_s[rows, :], v_ref[rows, :], kk_s[rows, :], beta_s[rows, :],
                              (lwf_s, lwb_s)[d][rows, :], consts_ref[d], d == 1))
        for u, (parts, decay) in enumerate(_rwkv_chunks_local(units)):
            j, d = divmod(u, 2)
            for idx, part in enumerate(parts):
                loc_s[d, j, idx] = part
            dec_s[d, j] = decay
        s = [st_ref[0], st_ref[1]]
        for j in range(group):
            parts = [[loc_s[d, j, idx] for idx in range(RWKV_LOCAL_PARTS)] for d in range(2)]
            ys, s = _rwkv_chunks_state(s, parts, [dec_s[d, j] for d in range(2)])
            for d in range(2):
                rows = rows_of(chunk_of(gi, j, d))
                y_s[rows, :] += ys[d]
        st_ref[0], st_ref[1] = s
        return carry

    y_s[...] = jnp.zeros(y_s.shape, F32)
    lax.fori_loop(0, nc // group, scan_group, 0)

    def finish(ci, carry):
        rows = rows_of(ci)
        y = y_s[rows, :]
        inv_n = 1.0 / HEAD_B
        mu = _mm_exact_rhs(y, seg) * inv_n
        dy = y - mu
        var = _mm_exact_rhs(dy * dy, seg) * inv_n
        yn = dy * lax.rsqrt(var + GN_EPS) * lnw_ref[...] + lnb_ref[...]
        v = v_ref[rows, :]
        bonus = _mm_exact_rhs(r_ref[rows, :] * kmod_s[rows, :] * rk_ref[...], seg) * v
        g = _mm(_sigmoid(lora_ref[rows, :]), wg_ref[...])
        o_ref[rows, :] = (yn + bonus) * g
        return carry

    lax.fori_loop(0, nc, finish, 0)
    for d in range(2):
        sout_ref[d, 0] = st_ref[d, 0:HEAD_B, 0:HEAD_B]
        sout_ref[d, 1] = st_ref[d, HEAD_B:LANES, HEAD_B:LANES]


def _rwkv_mixer(u_b, p, s0, seq_len):
    t = u_b.shape[0]
    nb = t // seq_len
    nc = seq_len // CHUNK
    group = min(RWKV_GROUP, nc)
    assert nc % group == 0
    pairs = N_HEADS_B // 2
    consts = jnp.asarray(_rwkv_consts(CHUNK), F32)
    lane = np.arange(LANES) // HEAD_B
    seg = jnp.asarray(lane[:, None] == lane[None, :], BF16)
    col = lambda part: pl.BlockSpec((seq_len, LANES), lambda b, h: (b, part * pairs + h))
    vec = lambda rows: pl.BlockSpec((rows, LANES), lambda b, h: (0, h))
    state_spec = pl.BlockSpec((None, 2, 2, HEAD_B, HEAD_B), lambda b, h: (b, 0, h, 0, 0))
    scr = lambda: pltpu.VMEM((seq_len, LANES), F32)
    return pl.pallas_call(
        functools.partial(_rwkv_kernel, nc=nc, group=group),
        grid=(nb, pairs),
        in_specs=[col(0), col(1), col(2),
                  pl.BlockSpec((seq_len, LORA_COLS), lambda b, h: (b, 3 * WIDTH_B // LORA_COLS)),
                  vec(2), vec(1), vec(1), vec(1), vec(1), vec(1), vec(1),
                  pl.BlockSpec((None, LORA_COLS, 2 * LANES), lambda b, h: (h, 0, 0)),
                  pl.BlockSpec((LORA_COLS, LANES), lambda b, h: (0, h)),
                  pl.BlockSpec((LORA_COLS, LANES), lambda b, h: (0, h)),
                  state_spec, _full(consts.shape), _full(seg.shape)],
        out_specs=[pl.BlockSpec((seq_len, LANES), lambda b, h: (b, h)), state_spec],
        out_shape=[jax.ShapeDtypeStruct((t, WIDTH_B), F32), jax.ShapeDtypeStruct((nb, 2, N_HEADS_B, HEAD_B, HEAD_B), F32)],
        scratch_shapes=[scr(), scr(), scr(), scr(), scr(), scr(),
                        pltpu.VMEM((2, group, RWKV_LOCAL_PARTS, LANES, LANES), F32),
                        pltpu.VMEM((2, group, 1, LANES), F32),
                        pltpu.VMEM((2, LANES, LANES), F32)],
        compiler_params=_params("parallel", "parallel"),
        name="rwkv7_scan",
    )(u_b, u_b, u_b, u_b, p["w0"], p["a0"], p["k_k"], p["k_a"], p["r_k"], p["ln_w"], p["ln_b"],
      p["w_dec"], p["w_a"], p["w_g"], s0, consts, seg)


def _mix_out_kernel(oa_ref, ob_ref, x_ref, mod_ref, woa_ref, wob_ref, nw_ref, wg_ref, wu_ref, x1_ref, g_ref, u_ref):
    mod = mod_ref[...]
    gate1 = mod[:, 2 * D_MODEL:3 * D_MODEL]
    shift2, scale2 = mod[:, 3 * D_MODEL:4 * D_MODEL], mod[:, 4 * D_MODEL:5 * D_MODEL]
    mix = _dot(oa_ref[...].astype(BF16), woa_ref[...]) + _dot(ob_ref[...].astype(BF16), wob_ref[...])
    x1 = x_ref[...] + gate1 * mix
    x1_ref[...] = x1
    h = _modulated_norm(x1, nw_ref[...], scale2, shift2).astype(BF16)
    g_ref[...] = _dot(h, wg_ref[...])
    u_ref[...] = _dot(h, wu_ref[...])


def _mix_out(o_a, o_b, x, mod, w_out_a, w_out_b, norm_w, w_gate, w_up, mod_of_tile):
    t = x.shape[0]
    row = lambda n: pl.BlockSpec((ROW_TILE, n), lambda i: (i, 0))
    return pl.pallas_call(
        _mix_out_kernel,
        grid=(t // ROW_TILE,),
        in_specs=[row(WIDTH_A), row(WIDTH_B), row(D_MODEL),
                  pl.BlockSpec((None, 1, 6 * D_MODEL), lambda i: (mod_of_tile(i), 0, 0)),
                  _full(w_out_a.shape), _full(w_out_b.shape), _full((1, D_MODEL)), _full(w_gate.shape), _full(w_up.shape)],
        out_specs=[row(D_MODEL), row(D_FF), row(D_FF)],
        out_shape=[jax.ShapeDtypeStruct((t, D_MODEL), F32), jax.ShapeDtypeStruct((t, D_FF), F32),
                   jax.ShapeDtypeStruct((t, D_FF), F32)],
        compiler_params=_params("parallel"),
        name="mix_out_ffn_in",
    )(o_a, o_b, x, mod, w_out_a, w_out_b, norm_w, w_gate, w_up)


def _gelu_tanh(x):
    return 0.5 * x * (1.0 + jnp.tanh(0.7978845608028654 * (x + 0.044715 * (x * x * x))))


def _ffn_out_kernel(g_ref, ga_ref, gb_ref, u_ref, x1_ref, mod_ref, cw_ref, cb_ref, wd_ref, fw_ref, y_ref,
                    *, seq_tiles, grid_conv):
    i = pl.program_id(0)
    cw = cw_ref[...]
    g = g_ref[...]
    n = g.shape[0]
    if grid_conv:
        top = (i % seq_tiles) == 0
        bottom = (i % seq_tiles) == seq_tiles - 1
        ext = jnp.concatenate([jnp.where(top, 0.0, ga_ref[...]), g, jnp.where(bottom, 0.0, gb_ref[...])], axis=0)
        width = GRID_W
        taps = (0, 1, 2)
    else:
        ext = g
        width = n
        taps = (1,)
    ne = ext.shape[0]
    col = lax.broadcasted_iota(jnp.int32, ext.shape, 0) % width
    left = jnp.where(col == 0, 0.0, pltpu.roll(ext, 1, 0))
    right = jnp.where(col == width - 1, 0.0, pltpu.roll(ext, ne - 1, 0))
    acc = None
    for dr in taps:
        off = (GRID_W * dr) if grid_conv else 0
        for dc, src in enumerate((left, ext, right)):
            term = cw[3 * dr + dc:3 * dr + dc + 1] * src[off:off + n]
            acc = term if acc is None else acc + term
    gt = acc + cb_ref[...]
    act = (_gelu_tanh(gt) * u_ref[...]).astype(BF16)
    mod = mod_ref[...]
    gate2 = mod[:, 5 * D_MODEL:6 * D_MODEL]
    x2 = x1_ref[...] + gate2 * _dot(act, wd_ref[...])
    y_ref[...] = x2 * lax.rsqrt(jnp.mean(x2 * x2, axis=-1, keepdims=True) + RMS_EPS) * fw_ref[...]


def _ffn_out(gpre, up, x1, mod, conv_w, conv_b, w_down, final_w, seq_tiles, grid_conv, mod_of_tile):
    t = x1.shape[0]
    nt = t // ROW_TILE
    per = ROW_TILE // GRID_W
    row = lambda n: pl.BlockSpec((ROW_TILE, n), lambda i: (i, 0))
    return pl.pallas_call(
        functools.partial(_ffn_out_kernel, seq_tiles=seq_tiles, grid_conv=grid_conv),
        grid=(nt,),
        in_specs=[row(D_FF),
                  pl.BlockSpec((GRID_W, D_FF), lambda i: (jnp.maximum(i * per - 1, 0), 0)),
                  pl.BlockSpec((GRID_W, D_FF), lambda i: (jnp.minimum((i + 1) * per, nt * per - 1), 0)),
                  row(D_FF), row(D_MODEL),
                  pl.BlockSpec((None, 1, 6 * D_MODEL), lambda i: (mod_of_tile(i), 0, 0)),
                  _full(conv_w.shape), _full((1, D_FF)), _full(w_down.shape), _full((1, D_MODEL))],
        out_specs=row(D_MODEL),
        out_shape=jax.ShapeDtypeStruct((t, D_MODEL), F32),
        compiler_params=_params("parallel"),
        name="ffn_out",
    )(gpre, gpre, gpre, up, x1, mod, conv_w, conv_b, w_down, final_w)


def _block(x, seq_len, mod, mod_of_tile, s_hgrn_t, s_rwkv, grid_conv, w):
    seq_tiles = seq_len // ROW_TILE
    u_a, u_b = _in_proj(x, mod, w["norm_mix_w"], w["w_in_a"], w["w_in_b"], w["rwkv_conv"], seq_tiles, mod_of_tile)
    o_a, s_h = _hgrn_mixer(u_a, w["hgrn_lb"], w["hgrn_norm_w"], s_hgrn_t, seq_len)
    o_b, s_r = _rwkv_mixer(u_b, w["rwkv"], s_rwkv, seq_len)
    x1, gpre, up = _mix_out(o_a, o_b, x, mod, w["w_out_a"], w["w_out_b"], w["norm_ffn_w"], w["ffn_w_gate"],
                            w["ffn_w_up"], mod_of_tile)
    y = _ffn_out(gpre, up, x1, mod, w["ffn_conv"], w["ffn_conv_b"], w["ffn_w_down"], w["final_norm_w"],
                 seq_tiles, grid_conv, mod_of_tile)
    return y, s_h, s_r


def _place_rows(w, start, total):
    return jnp.zeros((total, w.shape[1]), w.dtype).at[start:start + w.shape[0]].set(w)


def kernel(x_prompt, x_sample, state_hgrn, state_rwkv, c, c_ctx, ada_w, ada_b, norm_mix_w, w_in, hgrn_lb, hgrn_norm_w, rwkv_conv, rwkv_w0, rwkv_w2, rwkv_a0, rwkv_a2, rwkv_g2, rwkv_k_k, rwkv_k_a, rwkv_r_k, rwkv_ln_w, rwkv_ln_b, w_out, norm_ffn_w, ffn_w_gate, ffn_w_up, ffn_conv, ffn_conv_b, ffn_w_down, final_norm_w):
    assert w_in.shape[0] == 1, "one trunk layer"
    b_ctx, t_ctx, _ = x_prompt.shape
    b_lat, t_lat, _ = x_sample.shape
    row = lambda v: v.reshape(1, -1)
    pad_cols = P_B_PAD - P_B
    w_dec = jnp.concatenate([_place_rows(rwkv_w2[0, 0], 0, LORA_COLS), _place_rows(rwkv_w2[0, 1], LORA_W, LORA_COLS)], axis=1)
    pairs = N_HEADS_B // 2
    w_dec = w_dec.reshape(LORA_COLS, 2, pairs, LANES).transpose(2, 0, 1, 3).reshape(pairs, LORA_COLS, 2 * LANES)
    weights = dict(
        norm_mix_w=row(norm_mix_w[0]),
        w_in_a=w_in[0, :, :P_A].astype(BF16),
        w_in_b=jnp.pad(w_in[0, :, P_A:], ((0, 0), (0, pad_cols))).astype(BF16),
        rwkv_conv=jnp.pad(rwkv_conv[0], ((0, 0), (0, pad_cols))),
        hgrn_lb=hgrn_lb,
        hgrn_norm_w=row(hgrn_norm_w[0]),
        rwkv=dict(w0=rwkv_w0[0], a0=row(rwkv_a0[0]), k_k=row(rwkv_k_k[0]), k_a=row(rwkv_k_a[0]), r_k=row(rwkv_r_k[0]),
                  ln_w=row(rwkv_ln_w[0]), ln_b=row(rwkv_ln_b[0]), w_dec=w_dec,
                  w_a=_place_rows(rwkv_a2[0], 2 * LORA_W, LORA_COLS),
                  w_g=_place_rows(rwkv_g2[0], 2 * LORA_W + LORA_A, LORA_COLS)),
        w_out_a=w_out[0, :WIDTH_A].astype(BF16),
        w_out_b=w_out[0, WIDTH_A:].astype(BF16),
        norm_ffn_w=row(norm_ffn_w[0]),
        ffn_w_gate=ffn_w_gate[0].astype(BF16),
        ffn_w_up=ffn_w_up[0].astype(BF16),
        ffn_conv=ffn_conv[0].reshape(9, D_FF),
        ffn_conv_b=row(ffn_conv_b[0]),
        ffn_w_down=ffn_w_down[0].astype(BF16),
        final_norm_w=row(final_norm_w),
    )
    cvec = jnp.concatenate([c_ctx[None, :], c, jnp.zeros((8 - 1 - b_lat, D_MODEL), F32)], axis=0)
    mod = _modulation(cvec, ada_w[0], row(ada_b[0])).reshape(8, 1, 6 * D_MODEL)

    zeros_h = jnp.zeros((b_ctx, 2, N_HEADS_A, HEAD_A, HEAD_A), F32)
    zeros_r = jnp.zeros((b_ctx, 2, N_HEADS_B, HEAD_B, HEAD_B), F32)
    yp, s_h, s_r = _block(x_prompt.reshape(b_ctx * t_ctx, D_MODEL), t_ctx, mod, lambda i: 0, zeros_h, zeros_r, False, weights)
    lat_tiles = t_lat // ROW_TILE
    ys, _, _ = _block(x_sample.reshape(b_lat * t_lat, D_MODEL), t_lat, mod, lambda i: 1 + i // lat_tiles,
                      jnp.swapaxes(state_hgrn[:, 0], -1, -2), state_rwkv[:, 0], True, weights)
    y_prompt = yp.reshape(b_ctx, t_ctx, D_MODEL)
    y_sample = ys.reshape(b_lat, t_lat, D_MODEL)
    new_state_hgrn = jnp.swapaxes(s_h, -1, -2)[:, None]
    new_state_rwkv = s_r[:, None]
    return (y_prompt, y_sample, new_state_hgrn, new_state_rwkv)
```

```python
import functools

import numpy as np
import jax
import jax.numpy as jnp
from jax import lax
from jax.experimental import pallas as pl
from jax.experimental.pallas import tpu as pltpu

F32 = jnp.float32
BF16 = jnp.bfloat16

D_MODEL = 1024
GRID_W = 64
WIDTH_A = 512
HEAD_A = 128
N_HEADS_A = 4
WIDTH_B = 512
HEAD_B = 64
N_HEADS_B = 8
LORA_W = 32
LORA_A = 32
LORA_G = 96
D_FF = 2816
RMS_EPS = 1e-6
GN_EPS = 64e-5
DECAY_SCALE = 0.6065306597
P_A = 5 * WIDTH_A
P_B = 3 * WIDTH_B + 2 * LORA_W + LORA_A + LORA_G
LORA_COLS = 256
P_B_PAD = 3 * WIDTH_B + LORA_COLS

CHUNK = 64
LANES = 128
ROW_TILE = 256
VMEM_LIMIT = 56 * 1024 * 1024

NN = (((1,), (0,)), ((), ()))
NT = (((1,), (1,)), ((), ()))
TN = (((0,), (0,)), ((), ()))


def _dot(a, b, dims=NN):
    return lax.dot_general(a, b, dims, preferred_element_type=F32)


def _split(a):
    hi = a.astype(BF16)
    lo = (a - hi.astype(F32)).astype(BF16)
    return hi, lo


def _mm(a, b, dims=NN, passes=1):
    if passes == 1:
        return _dot(a.astype(BF16), b.astype(BF16), dims)
    ah, al = _split(a)
    bh, bl = _split(b)
    return _dot(ah, bh, dims) + (_dot(ah, bl, dims) + _dot(al, bh, dims))


def _mm_exact_lhs(m, x):
    n = x.shape[1]
    both = _dot(m, jnp.concatenate(_split(x), axis=1))
    return both[:, :n] + both[:, n:]


def _sigmoid(x):
    return 1.0 / (1.0 + jnp.exp(-x))


def _silu(x):
    return x * _sigmoid(x)


def _hgrn_level_consts(c, rev):
    t = np.arange(c)
    mats = [(t[None, :] <= t[:, None]).astype(np.float32)]
    masks = [np.eye(c, dtype=np.float32)]
    n = 1
    while n < c:
        blk, half = t // (2 * n), (t // n) % 2
        mid = blk * 2 * n + n
        m = np.zeros((c, c), np.float32)
        for row in range(c):
            if half[row] == 1:
                m[row, mid[row]:row + 1] = 1.0
            else:
                m[row, row + 1:mid[row]] = 1.0
        mats.append(m)
        masks.append(((blk[:, None] == blk[None, :]) & (half[:, None] == 1) & (half[None, :] == 0))
                     .astype(np.float32))
        n *= 2
    if rev:
        mats = [m[::-1, ::-1] for m in mats]
        masks = [m[::-1, ::-1] for m in masks]
    return np.concatenate(mats, 0), np.concatenate(masks, 0)


def _hgrn_consts(c):
    mf, kf = _hgrn_level_consts(c, False)
    mb, kb = _hgrn_level_consts(c, True)
    return np.stack([mf, mb]), np.stack([kf, kb])


def _rwkv_consts(c):
    t = np.arange(c)
    out = []
    for rev in (False, True):
        incl = (t[None, :] <= t[:, None]) if not rev else (t[None, :] >= t[:, None])
        strict = (t[None, :] < t[:, None]) if not rev else (t[None, :] > t[:, None])
        z = np.zeros((c, c), bool)
        cum = np.block([[incl, z], [z, z]])
        out.append(np.stack([cum, np.block([[strict, z], [z, strict]]), np.block([[incl, z], [z, incl]])]))
    return np.stack(out).astype(np.float32)


def _each(fn, *cols):
    return [fn(*args) for args in zip(*cols)]


def _interleave(*gens):
    results = [None] * len(gens)
    live = dict(enumerate(gens))
    while live:
        for i in list(live):
            try:
                next(live[i])
            except StopIteration as stop:
                results[i] = stop.value
                del live[i]
    return results


def _hgrn_chunks_local(units):
    q, k, v, g, m, masks, rev = (list(col) for col in zip(*units))
    c = q[0].shape[0]
    levels = m[0].shape[0] // c
    e_all = _each(_mm_exact_lhs, m, g)
    yield
    b = [x[:c] for x in e_all]
    edge = _each(lambda x, rv: x[0:1] if rv else x[c - 1:c], b, rev)
    kv = _each(lambda vv, kk, ed, bb: _mm(vv, kk * jnp.exp(ed - bb), TN), v, k, edge, b)
    yield
    sc = _each(lambda mk, a, kk: mk[:c] * _mm(a, kk, NT), masks, q, k)
    yield
    for l in range(1, levels):
        e = _each(lambda x: jnp.exp(x[l * c:(l + 1) * c]), e_all)
        sc = _each(lambda s, mk, a, kk, ee: s + mk[l * c:(l + 1) * c] * _mm(a * ee, kk * ee, NT), sc, masks, q, k, e)
        yield
    intra = _each(_mm, sc, v)
    yield
    return [(ii, qq * jnp.exp(bb), kvv, jnp.exp(ed)) for ii, qq, bb, kvv, ed in zip(intra, q, b, kv, edge)]


def _hgrn_chunk_state(st, part):
    intra, qb, kv, decay = part
    return intra + _mm(qb, st, NT), st * decay + kv


def _stack_heads(x, lane_head):
    return jnp.concatenate([jnp.where(lane_head == 0, x, 0.0), jnp.where(lane_head == 1, x, 0.0)], axis=0)


RWKV_PASSES_SCORE = 1
RWKV_PASSES_SOLVE = 1
RWKV_PASSES_STATE = 1
RWKV_LOCAL_PARTS = 7
RWKV_UNITS = 8
HGRN_GROUP = 4
MIXER_ROW_BLOCK = 256


def _rwkv_chunks_local(units):
    c = units[0][0].shape[0]
    lane_head = lax.broadcasted_iota(jnp.int32, (c, LANES), 1) // HEAD_B
    st = lambda x: _stack_heads(x, lane_head)
    r, k, v, kk, beta, lw, consts, rev = (list(col) for col in zip(*units))
    strict = [cs[1] for cs in consts]
    incl = [cs[2] for cs in consts]
    g = _each(lambda cs, x: _mm_exact_lhs(cs[0][:c, :c].astype(BF16), x), consts, lw)
    edge = _each(lambda x, rv: x[0:1] if rv else x[c - 1:c], g, rev)
    yield
    kg = _each(lambda x, gg, l: st(x * jnp.exp(gg - l)), kk, g, lw)
    rg = _each(lambda x, gg: st(x * jnp.exp(gg)), r, g)
    e_out = _each(lambda gg: jnp.exp(-gg), g)
    bi = _each(lambda x, e: st(x * e), beta, e_out)
    ki = _each(lambda x, e: st(x * e), k, e_out)
    vs = _each(st, v)
    a_all = _each(lambda a, b, cc, d: _mm(jnp.concatenate([a, b], 0), jnp.concatenate([cc, d], 0), NT, RWKV_PASSES_SCORE),
                  kg, rg, bi, ki)
    yield
    n = _each(lambda m, a: m * a[:2 * c, :2 * c], strict, a_all)
    a_ak = _each(lambda m, a: m * a[:2 * c, 2 * c:], strict, a_all)
    a_rb = _each(lambda m, a: m * a[2 * c:, :2 * c], incl, a_all)
    a_rk = _each(lambda m, a: m * a[2 * c:, 2 * c:], incl, a_all)
    av = _each(lambda a, b, x: _mm(jnp.concatenate([a, b], 0), x, passes=RWKV_PASSES_SCORE), a_ak, a_rk, vs)
    yield
    x = _each(lambda a, b: jnp.concatenate([a, b[:2 * c]], axis=1), kg, av)
    powers = [n]
    while (1 << len(powers)) < c:
        powers.append(_each(lambda p: _mm(p, p, passes=RWKV_PASSES_SOLVE), powers[-1]))
        yield
    for p in reversed(powers[1:]):
        x = _each(lambda pp, xx: xx + _mm(pp, xx, passes=RWKV_PASSES_SOLVE), p, x)
        yield
    x = _each(lambda pp, xx: xx - _mm(pp, xx, passes=RWKV_PASSES_SOLVE), n, x)
    yield
    e_edge = _each(lambda ed, gg: jnp.exp(ed - gg), edge, g)
    kv = _each(lambda a, b, e: _mm(a, st(b * e), TN, RWKV_PASSES_STATE), vs, k, e_edge)
    yield
    bgc = _each(lambda b, e: st(b * e), beta, e_edge)
    return [((xx[:, :LANES], rr, xx[:, LANES:], ab, a[2 * c:], bg, kvv), jnp.exp(ed))
            for xx, rr, ab, a, bg, kvv, ed in zip(x, rg, a_rb, av, bgc, kv, edge)]


def _rwkv_chunks_state(states, parts, decays):
    w1, rg, x2, a_rb, y0, bgc, kv = (list(col) for col in zip(*parts))
    c2 = w1[0].shape[0]
    hs = _each(lambda a, b, s: _mm(jnp.concatenate([a, b], 0), s, NT, RWKV_PASSES_STATE), w1, rg, states)
    yield
    u = _each(lambda h, x: -(h[:c2] + x), hs, x2)
    s = _each(lambda s_, d, uu, b, kv_: s_ * d + _mm(uu, b, TN, RWKV_PASSES_STATE) + kv_, states, decays, u, bgc, kv)
    yield
    y = _each(lambda h, y_, a, uu: h[c2:] + y_ + _mm(a, uu, passes=RWKV_PASSES_STATE), hs, y0, a_rb, u)
    yield
    return [yy[:c2 // 2] + yy[c2 // 2:] for yy in y], s


def _params(*semantics):
    return pltpu.CompilerParams(dimension_semantics=semantics, vmem_limit_bytes=VMEM_LIMIT)


def _full(shape):
    return pl.BlockSpec(shape, lambda *_: (0,) * len(shape))


def _modulated_norm(x, norm_w, scale, shift):
    y = x * lax.rsqrt(jnp.mean(x * x, axis=-1, keepdims=True) + RMS_EPS)
    return (y * norm_w) * (1.0 + scale) + shift


def _mod_kernel(c_ref, w_ref, b_ref, o_ref):
    o_ref[...] = _mm(_silu(c_ref[...]), w_ref[...], passes=3) + b_ref[...]


def _modulation(cvec, ada_w, ada_b):
    n = ada_w.shape[1]
    tn = n // 4
    return pl.pallas_call(
        _mod_kernel,
        grid=(n // tn,),
        in_specs=[_full(cvec.shape), pl.BlockSpec((D_MODEL, tn), lambda j: (0, j)), pl.BlockSpec((1, tn), lambda j: (0, j))],
        out_specs=pl.BlockSpec((cvec.shape[0], tn), lambda j: (0, j)),
        out_shape=jax.ShapeDtypeStruct((cvec.shape[0], n), F32),
        compiler_params=_params("arbitrary"),
        name="modulation",
    )(cvec, ada_w, ada_b)


def _in_proj_kernel(x_ref, xp_ref, xn_ref, mod_ref, nw_ref, wa_ref, wb_ref, cw_ref, ua_ref, ub_ref, *, seq_tiles):
    i = pl.program_id(0)
    mod = mod_ref[...]
    shift, scale = mod[:, 0:D_MODEL], mod[:, D_MODEL:2 * D_MODEL]
    nw = nw_ref[...]
    h = _modulated_norm(x_ref[...], nw, scale, shift).astype(BF16)
    ua_ref[...] = _dot(h, wa_ref[...])
    ub = _dot(h, wb_ref[...])
    halo = jnp.concatenate([xp_ref[...], xn_ref[...]], axis=0)
    hh = _modulated_norm(halo, nw, scale, shift).astype(BF16)
    ubh = _dot(hh, wb_ref[...])
    first = (i % seq_tiles) == 0
    last = (i % seq_tiles) == seq_tiles - 1
    prev_row = jnp.where(first, 0.0, ubh[7:8])
    next_row = jnp.where(last, 0.0, ubh[8:9])
    rows = lax.broadcasted_iota(jnp.int32, ub.shape, 0)
    n = ub.shape[0]
    below = jnp.where(rows == 0, prev_row, pltpu.roll(ub, 1, 0))
    above = jnp.where(rows == n - 1, next_row, pltpu.roll(ub, n - 1, 0))
    cw = cw_ref[...]
    ub_ref[...] = cw[0:1] * below + cw[1:2] * ub + cw[2:3] * above


def _in_proj(x, mod, norm_w, w_a, w_b, conv_w, seq_tiles, mod_of_tile):
    t = x.shape[0]
    nt = t // ROW_TILE
    sub = ROW_TILE // 8
    return pl.pallas_call(
        functools.partial(_in_proj_kernel, seq_tiles=seq_tiles),
        grid=(nt,),
        in_specs=[
            pl.BlockSpec((ROW_TILE, D_MODEL), lambda i: (i, 0)),
            pl.BlockSpec((8, D_MODEL), lambda i: (jnp.maximum(i * sub - 1, 0), 0)),
            pl.BlockSpec((8, D_MODEL), lambda i: (jnp.minimum((i + 1) * sub, nt * sub - 1), 0)),
            pl.BlockSpec((None, 1, 6 * D_MODEL), lambda i: (mod_of_tile(i), 0, 0)),
            _full((1, D_MODEL)), _full(w_a.shape), _full(w_b.shape), _full(conv_w.shape),
        ],
        out_specs=[pl.BlockSpec((ROW_TILE, P_A), lambda i: (i, 0)), pl.BlockSpec((ROW_TILE, P_B_PAD), lambda i: (i, 0))],
        out_shape=[jax.ShapeDtypeStruct((t, P_A), F32), jax.ShapeDtypeStruct((t, P_B_PAD), F32)],
        compiler_params=_params("parallel"),
        name="in_proj",
    )(x, x, x, mod, norm_w, w_a, w_b, conv_w)


def _hgrn_kernel(q_ref, i_ref, zf_ref, zb_ref, g_ref, lb_ref, nw_ref, s0_ref, m_ref, mask_ref,
                 o_ref, sout_ref, acc_s, st_ref, *, nc, group, block):
    c = CHUNK
    p0, p1 = lb_ref[0], lb_ref[1]
    mx = jnp.maximum(p0, p1)
    e0, e1 = jnp.exp(p0 - mx), jnp.exp(p1 - mx)
    lb = e0 / (e0 + e1)
    st_ref[...] = s0_ref[...]

    def rows_of(ci):
        return pl.ds(pl.multiple_of(ci * c, c), c)

    def chunk_of(gi, j, d):
        cf = gi * group + j
        return cf if d == 0 else nc - 1 - cf

    def scan_group(gi, carry):
        units = []
        for j in range(group):
            for d in range(2):
                rows = rows_of(chunk_of(gi, j, d))
                lo = lb[d:d + 1]
                f = lo + (1.0 - lo) * _sigmoid((zf_ref, zb_ref)[d][rows, :])
                units.append((_silu(q_ref[rows, :]), 1.0 - f, i_ref[rows, :], jnp.log(f), m_ref[d], mask_ref[d], d == 1))
        parts, = _interleave(_hgrn_chunks_local(units))
        st = [st_ref[0], st_ref[1]]
        for j in range(group):
            for d in range(2):
                o, st[d] = _hgrn_chunk_state(st[d], parts[2 * j + d])
                acc_s[rows_of(chunk_of(gi, j, d)), :] += o
        st_ref[0], st_ref[1] = st
        return carry

    acc_s[...] = jnp.zeros(acc_s.shape, F32)
    lax.fori_loop(0, nc // group, scan_group, 0)

    def finish(bi, carry):
        rows = pl.ds(pl.multiple_of(bi * block, block), block)
        o = acc_s[rows, :]
        o = o * lax.rsqrt(jnp.mean(o * o, axis=-1, keepdims=True) + RMS_EPS) * nw_ref[...]
        o_ref[rows, :] = o * _silu(g_ref[rows, :])
        return carry

    lax.fori_loop(0, nc * c // block, finish, 0)
    sout_ref[...] = st_ref[...]


def _hgrn_mixer(u_a, lb_raw, norm_w, s0t, seq_len):
    t = u_a.shape[0]
    nb = t // seq_len
    nc = seq_len // CHUNK
    group = min(HGRN_GROUP, nc)
    assert nc % group == 0
    mats, masks = _hgrn_consts(CHUNK)
    mats = jnp.asarray(mats, BF16)
    masks = jnp.asarray(masks, F32)
    col = lambda part: pl.BlockSpec((seq_len, HEAD_A), lambda b, h: (b, part * N_HEADS_A + h))
    state_spec = pl.BlockSpec((None, 2, None, HEAD_A, HEAD_A), lambda b, h: (b, 0, h, 0, 0))
    return pl.pallas_call(
        functools.partial(_hgrn_kernel, nc=nc, group=group, block=min(MIXER_ROW_BLOCK, seq_len)),
        grid=(nb, N_HEADS_A),
        in_specs=[col(0), col(1), col(2), col(3), col(4),
                  pl.BlockSpec((2, 2, HEAD_A), lambda b, h: (0, 0, h)),
                  pl.BlockSpec((1, HEAD_A), lambda b, h: (0, h)),
                  state_spec, _full(mats.shape), _full(masks.shape)],
        out_specs=[pl.BlockSpec((seq_len, HEAD_A), lambda b, h: (b, h)), state_spec],
        out_shape=[jax.ShapeDtypeStruct((t, WIDTH_A), F32), jax.ShapeDtypeStruct((nb, 2, N_HEADS_A, HEAD_A, HEAD_A), F32)],
        scratch_shapes=[pltpu.VMEM((seq_len, HEAD_A), F32), pltpu.VMEM((2, HEAD_A, HEAD_A), F32)],
        compiler_params=_params("parallel", "parallel"),
        name="hgrn2_scan",
    )(u_a, u_a, u_a, u_a, u_a, lb_raw, norm_w, s0t, mats, masks)


def _mm_exact_rhs(x, m):
    n = x.shape[0]
    both = _dot(jnp.concatenate(_split(x), axis=0), m)
    return both[:n] + both[n:]


def _rwkv_kernel(r_ref, k_ref, v_ref, lora_ref, w0_ref, a0_ref, kkw_ref, ka_ref, rk_ref, lnw_ref, lnb_ref,
                 wdec_ref, wa_ref, wg_ref, s0_ref, consts_ref, seg_ref,
                 o_ref, sout_ref, lwf_s, lwb_s, kk_s, beta_s, kmod_s, y_s, loca_s, deca_s, locb_s, decb_s, st_ref,
                 *, ns, nc, group, block):
    c = CHUNK
    seq_len = nc * c
    chains = [(s, d) for s in range(ns) for d in range(2)]
    seg = seg_ref[...]

    for ch, (s, d) in enumerate(chains):
        st_ref[ch] = jnp.zeros((LANES, LANES), F32)
        st_ref[ch, 0:HEAD_B, 0:HEAD_B] = s0_ref[s, d, 0]
        st_ref[ch, HEAD_B:LANES, HEAD_B:LANES] = s0_ref[s, d, 1]

    def block_of(bi):
        return pl.ds(pl.multiple_of(bi * block, block), block)

    def prepare(bi, carry):
        rows = block_of(bi)
        lora = lora_ref[rows, :]
        dec = _mm(jnp.tanh(lora), wdec_ref[...], passes=3)
        w0 = w0_ref[...]
        lwf_s[rows, :] = -DECAY_SCALE * _sigmoid(w0[0:1] + dec[:, :LANES])
        lwb_s[rows, :] = -DECAY_SCALE * _sigmoid(w0[1:2] + dec[:, LANES:])
        a = _sigmoid(a0_ref[...] + _mm(lora, wa_ref[...]))
        k = k_ref[rows, :]
        kk = k * kkw_ref[...]
        kk = kk * lax.rsqrt(_mm_exact_rhs(kk * kk, seg) + 1e-12)
        kk_s[rows, :] = kk
        beta_s[rows, :] = kk * a
        kmod_s[rows, :] = k * (1.0 + (a - 1.0) * ka_ref[...])
        return carry

    lax.fori_loop(0, ns * seq_len // block, prepare, 0)

    def rows_of(gi, j, s, d):
        cf = gi * group + j
        ci = cf if d == 0 else nc - 1 - cf
        return pl.ds(pl.multiple_of(s * seq_len + ci * c, c), c)

    def local_stage(gi, loc, dec):
        units = []
        for j in range(group):
            for s, d in chains:
                rows = rows_of(gi, j, s, d)
                units.append((r_ref[rows, :], kmod_s[rows, :], v_ref[rows, :], kk_s[rows, :], beta_s[rows, :],
                              (lwf_s, lwb_s)[d][rows, :], consts_ref[d], d == 1))
        results = yield from _rwkv_chunks_local(units)
        for u, (parts, decay) in enumerate(results):
            j, ch = divmod(u, len(chains))
            for idx, part in enumerate(parts):
                loc[ch, j, idx] = part
            dec[ch, j] = decay

    def state_stage(gi, loc, dec):
        states = [st_ref[ch] for ch in range(len(chains))]
        for j in range(group):
            parts = [[loc[ch, j, idx] for idx in range(RWKV_LOCAL_PARTS)] for ch in range(len(chains))]
            ys, states = yield from _rwkv_chunks_state(states, parts, [dec[ch, j] for ch in range(len(chains))])
            for y, (s, d) in zip(ys, chains):
                y_s[rows_of(gi, j, s, d), :] += y
        for ch, state in enumerate(states):
            st_ref[ch] = state

    y_s[...] = jnp.zeros(y_s.shape, F32)
    ng = nc // group
    _interleave(local_stage(0, loca_s, deca_s))

    def two_groups(p, carry):
        _interleave(local_stage(2 * p + 1, locb_s, decb_s), state_stage(2 * p, loca_s, deca_s))
        _interleave(local_stage(2 * p + 2, loca_s, deca_s), state_stage(2 * p + 1, locb_s, decb_s))
        return carry

    lax.fori_loop(0, ng // 2 - 1, two_groups, 0)
    _interleave(local_stage(ng - 1, locb_s, decb_s), state_stage(ng - 2, loca_s, deca_s))
    _interleave(state_stage(ng - 1, locb_s, decb_s))

    def finish(bi, carry):
        rows = block_of(bi)
        y = y_s[rows, :]
        inv_n = 1.0 / HEAD_B
        mu = _mm_exact_rhs(y, seg) * inv_n
        dy = y - mu
        var = _mm_exact_rhs(dy * dy, seg) * inv_n
        yn = dy * lax.rsqrt(var + GN_EPS) * lnw_ref[...] + lnb_ref[...]
        v = v_ref[rows, :]
        bonus = _mm_exact_rhs(r_ref[rows, :] * kmod_s[rows, :] * rk_ref[...], seg) * v
        g = _mm(_sigmoid(lora_ref[rows, :]), wg_ref[...])
        o_ref[rows, :] = (yn + bonus) * g
        return carry

    lax.fori_loop(0, ns * seq_len // block, finish, 0)
    for ch, (s, d) in enumerate(chains):
        sout_ref[s, d, 0] = st_ref[ch, 0:HEAD_B, 0:HEAD_B]
        sout_ref[s, d, 1] = st_ref[ch, HEAD_B:LANES, HEAD_B:LANES]


def _rwkv_mixer(u_b, p, s0, seq_len):
    t = u_b.shape[0]
    nb = t // seq_len
    nc = seq_len // CHUNK
    group = max(1, min(RWKV_UNITS // 2, nc // 4))
    ns = max(1, min(RWKV_UNITS // (2 * group), nb))
    assert nc % (2 * group) == 0 and nb % ns == 0
    rows = ns * seq_len
    block = min(MIXER_ROW_BLOCK, rows)
    pairs = N_HEADS_B // 2
    consts = jnp.asarray(_rwkv_consts(CHUNK), F32)
    lane = np.arange(LANES) // HEAD_B
    seg = jnp.asarray(lane[:, None] == lane[None, :], BF16)
    col = lambda part: pl.BlockSpec((rows, LANES), lambda b, h: (b, part * pairs + h))
    vec = lambda n: pl.BlockSpec((n, LANES), lambda b, h: (0, h))
    state_spec = pl.BlockSpec((ns, 2, 2, HEAD_B, HEAD_B), lambda b, h: (b, 0, h, 0, 0))
    scr = lambda: pltpu.VMEM((rows, LANES), F32)
    loc = lambda: pltpu.VMEM((2 * ns, group, RWKV_LOCAL_PARTS, LANES, LANES), F32)
    dec = lambda: pltpu.VMEM((2 * ns, group, 1, LANES), F32)
    return pl.pallas_call(
        functools.partial(_rwkv_kernel, ns=ns, nc=nc, group=group, block=block),
        grid=(nb // ns, pairs),
        in_specs=[col(0), col(1), col(2),
                  pl.BlockSpec((rows, LORA_COLS), lambda b, h: (b, 3 * WIDTH_B // LORA_COLS)),
                  vec(2), vec(1), vec(1), vec(1), vec(1), vec(1), vec(1),
                  pl.BlockSpec((None, LORA_COLS, 2 * LANES), lambda b, h: (h, 0, 0)),
                  pl.BlockSpec((LORA_COLS, LANES), lambda b, h: (0, h)),
                  pl.BlockSpec((LORA_COLS, LANES), lambda b, h: (0, h)),
                  state_spec, _full(consts.shape), _full(seg.shape)],
        out_specs=[pl.BlockSpec((rows, LANES), lambda b, h: (b, h)), state_spec],
        out_shape=[jax.ShapeDtypeStruct((t, WIDTH_B), F32), jax.ShapeDtypeStruct((nb, 2, N_HEADS_B, HEAD_B, HEAD_B), F32)],
        scratch_shapes=[scr(), scr(), scr(), scr(), scr(), scr(), loc(), dec(), loc(), dec(),
                        pltpu.VMEM((2 * ns, LANES, LANES), F32)],
        compiler_params=_params("parallel", "parallel"),
        name="rwkv7_scan",
    )(u_b, u_b, u_b, u_b, p["w0"], p["a0"], p["k_k"], p["k_a"], p["r_k"], p["ln_w"], p["ln_b"],
      p["w_dec"], p["w_a"], p["w_g"], s0, consts, seg)


def _mix_out_kernel(oa_ref, ob_ref, x_ref, mod_ref, woa_ref, wob_ref, nw_ref, wg_ref, wu_ref, x1_ref, g_ref, u_ref):
    mod = mod_ref[...]
    gate1 = mod[:, 2 * D_MODEL:3 * D_MODEL]
    shift2, scale2 = mod[:, 3 * D_MODEL:4 * D_MODEL], mod[:, 4 * D_MODEL:5 * D_MODEL]
    mix = _dot(oa_ref[...].astype(BF16), woa_ref[...]) + _dot(ob_ref[...].astype(BF16), wob_ref[...])
    x1 = x_ref[...] + gate1 * mix
    x1_ref[...] = x1
    h = _modulated_norm(x1, nw_ref[...], scale2, shift2).astype(BF16)
    g_ref[...] = _dot(h, wg_ref[...])
    u_ref[...] = _dot(h, wu_ref[...])


def _mix_out(o_a, o_b, x, mod, w_out_a, w_out_b, norm_w, w_gate, w_up, mod_of_tile):
    t = x.shape[0]
    row = lambda n: pl.BlockSpec((ROW_TILE, n), lambda i: (i, 0))
    return pl.pallas_call(
        _mix_out_kernel,
        grid=(t // ROW_TILE,),
        in_specs=[row(WIDTH_A), row(WIDTH_B), row(D_MODEL),
                  pl.BlockSpec((None, 1, 6 * D_MODEL), lambda i: (mod_of_tile(i), 0, 0)),
                  _full(w_out_a.shape), _full(w_out_b.shape), _full((1, D_MODEL)), _full(w_gate.shape), _full(w_up.shape)],
        out_specs=[row(D_MODEL), row(D_FF), row(D_FF)],
        out_shape=[jax.ShapeDtypeStruct((t, D_MODEL), F32), jax.ShapeDtypeStruct((t, D_FF), F32),
                   jax.ShapeDtypeStruct((t, D_FF), F32)],
        compiler_params=_params("parallel"),
        name="mix_out_ffn_in",
    )(o_a, o_b, x, mod, w_out_a, w_out_b, norm_w, w_gate, w_up)


def _gelu_tanh(x):
    return 0.5 * x * (1.0 + jnp.tanh(0.7978845608028654 * (x + 0.044715 * (x * x * x))))


def _ffn_out_kernel(g_ref, ga_ref, gb_ref, u_ref, x1_ref, mod_ref, cw_ref, cb_ref, wd_ref, fw_ref, y_ref,
                    *, seq_tiles, grid_conv):
    i = pl.program_id(0)
    cw = cw_ref[...]
    g = g_ref[...]
    n = g.shape[0]
    if grid_conv:
        top = (i % seq_tiles) == 0
        bottom = (i % seq_tiles) == seq_tiles - 1
        ext = jnp.concatenate([jnp.where(top, 0.0, ga_ref[...]), g, jnp.where(bottom, 0.0, gb_ref[...])], axis=0)
        width = GRID_W
        taps = (0, 1, 2)
    else:
        ext = g
        width = n
        taps = (1,)
    ne = ext.shape[0]
    col = lax.broadcasted_iota(jnp.int32, ext.shape, 0) % width
    left = jnp.where(col == 0, 0.0, pltpu.roll(ext, 1, 0))
    right = jnp.where(col == width - 1, 0.0, pltpu.roll(ext, ne - 1, 0))
    acc = None
    for dr in taps:
        off = (GRID_W * dr) if grid_conv else 0
        for dc, src in enumerate((left, ext, right)):
            term = cw[3 * dr + dc:3 * dr + dc + 1] * src[off:off + n]
            acc = term if acc is None else acc + term
    gt = acc + cb_ref[...]
    act = (_gelu_tanh(gt) * u_ref[...]).astype(BF16)
    mod = mod_ref[...]
    gate2 = mod[:, 5 * D_MODEL:6 * D_MODEL]
    x2 = x1_ref[...] + gate2 * _dot(act, wd_ref[...])
    y_ref[...] = x2 * lax.rsqrt(jnp.mean(x2 * x2, axis=-1, keepdims=True) + RMS_EPS) * fw_ref[...]


def _ffn_out(gpre, up, x1, mod, conv_w, conv_b, w_down, final_w, seq_tiles, grid_conv, mod_of_tile):
    t = x1.shape[0]
    nt = t // ROW_TILE
    per = ROW_TILE // GRID_W
    row = lambda n: pl.BlockSpec((ROW_TILE, n), lambda i: (i, 0))
    return pl.pallas_call(
        functools.partial(_ffn_out_kernel, seq_tiles=seq_tiles, grid_conv=grid_conv),
        grid=(nt,),
        in_specs=[row(D_FF),
                  pl.BlockSpec((GRID_W, D_FF), lambda i: (jnp.maximum(i * per - 1, 0), 0)),
                  pl.BlockSpec((GRID_W, D_FF), lambda i: (jnp.minimum((i + 1) * per, nt * per - 1), 0)),
                  row(D_FF), row(D_MODEL),
                  pl.BlockSpec((None, 1, 6 * D_MODEL), lambda i: (mod_of_tile(i), 0, 0)),
                  _full(conv_w.shape), _full((1, D_FF)), _full(w_down.shape), _full((1, D_MODEL))],
        out_specs=row(D_MODEL),
        out_shape=jax.ShapeDtypeStruct((t, D_MODEL), F32),
        compiler_params=_params("parallel"),
        name="ffn_out",
    )(gpre, gpre, gpre, up, x1, mod, conv_w, conv_b, w_down, final_w)


def _block(x, seq_len, mod, mod_of_tile, s_hgrn_t, s_rwkv, grid_conv, w):
    seq_tiles = seq_len // ROW_TILE
    u_a, u_b = _in_proj(x, mod, w["norm_mix_w"], w["w_in_a"], w["w_in_b"], w["rwkv_conv"], seq_tiles, mod_of_tile)
    o_a, s_h = _hgrn_mixer(u_a, w["hgrn_lb"], w["hgrn_norm_w"], s_hgrn_t, seq_len)
    o_b, s_r = _rwkv_mixer(u_b, w["rwkv"], s_rwkv, seq_len)
    x1, gpre, up = _mix_out(o_a, o_b, x, mod, w["w_out_a"], w["w_out_b"], w["norm_ffn_w"], w["ffn_w_gate"],
                            w["ffn_w_up"], mod_of_tile)
    y = _ffn_out(gpre, up, x1, mod, w["ffn_conv"], w["ffn_conv_b"], w["ffn_w_down"], w["final_norm_w"],
                 seq_tiles, grid_conv, mod_of_tile)
    return y, s_h, s_r


def _place_rows(w, start, total):
    return jnp.zeros((total, w.shape[1]), w.dtype).at[start:start + w.shape[0]].set(w)


def kernel(x_prompt, x_sample, state_hgrn, state_rwkv, c, c_ctx, ada_w, ada_b, norm_mix_w, w_in, hgrn_lb, hgrn_norm_w, rwkv_conv, rwkv_w0, rwkv_w2, rwkv_a0, rwkv_a2, rwkv_g2, rwkv_k_k, rwkv_k_a, rwkv_r_k, rwkv_ln_w, rwkv_ln_b, w_out, norm_ffn_w, ffn_w_gate, ffn_w_up, ffn_conv, ffn_conv_b, ffn_w_down, final_norm_w):
    assert w_in.shape[0] == 1, "one trunk layer"
    b_ctx, t_ctx, _ = x_prompt.shape
    b_lat, t_lat, _ = x_sample.shape
    row = lambda v: v.reshape(1, -1)
    pad_cols = P_B_PAD - P_B
    w_dec = jnp.concatenate([_place_rows(rwkv_w2[0, 0], 0, LORA_COLS), _place_rows(rwkv_w2[0, 1], LORA_W, LORA_COLS)], axis=1)
    pairs = N_HEADS_B // 2
    w_dec = w_dec.reshape(LORA_COLS, 2, pairs, LANES).transpose(2, 0, 1, 3).reshape(pairs, LORA_COLS, 2 * LANES)
    weights = dict(
        norm_mix_w=row(norm_mix_w[0]),
        w_in_a=w_in[0, :, :P_A].astype(BF16),
        w_in_b=jnp.pad(w_in[0, :, P_A:], ((0, 0), (0, pad_cols))).astype(BF16),
        rwkv_conv=jnp.pad(rwkv_conv[0], ((0, 0), (0, pad_cols))),
        hgrn_lb=hgrn_lb,
        hgrn_norm_w=row(hgrn_norm_w[0]),
        rwkv=dict(w0=rwkv_w0[0], a0=row(rwkv_a0[0]), k_k=row(rwkv_k_k[0]), k_a=row(rwkv_k_a[0]), r_k=row(rwkv_r_k[0]),
                  ln_w=row(rwkv_ln_w[0]), ln_b=row(rwkv_ln_b[0]), w_dec=w_dec,
                  w_a=_place_rows(rwkv_a2[0], 2 * LORA_W, LORA_COLS),
                  w_g=_place_rows(rwkv_g2[0], 2 * LORA_W + LORA_A, LORA_COLS)),
        w_out_a=w_out[0, :WIDTH_A].astype(BF16),
        w_out_b=w_out[0, WIDTH_A:].astype(BF16),
        norm_ffn_w=row(norm_ffn_w[0]),
        ffn_w_gate=ffn_w_gate[0].astype(BF16),
        ffn_w_up=ffn_w_up[0].astype(BF16),
        ffn_conv=ffn_conv[0].reshape(9, D_FF),
        ffn_conv_b=row(ffn_conv_b[0]),
        ffn_w_down=ffn_w_down[0].astype(BF16),
        final_norm_w=row(final_norm_w),
    )
    cvec = jnp.concatenate([c_ctx[None, :], c, jnp.zeros((8 - 1 - b_lat, D_MODEL), F32)], axis=0)
    mod = _modulation(cvec, ada_w[0], row(ada_b[0])).reshape(8, 1, 6 * D_MODEL)

    zeros_h = jnp.zeros((b_ctx, 2, N_HEADS_A, HEAD_A, HEAD_A), F32)
    zeros_r = jnp.zeros((b_ctx, 2, N_HEADS_B, HEAD_B, HEAD_B), F32)
    yp, s_h, s_r = _block(x_prompt.reshape(b_ctx * t_ctx, D_MODEL), t_ctx, mod, lambda i: 0, zeros_h, zeros_r, False, weights)
    lat_tiles = t_lat // ROW_TILE
    ys, _, _ = _block(x_sample.reshape(b_lat * t_lat, D_MODEL), t_lat, mod, lambda i: 1 + i // lat_tiles,
                      jnp.swapaxes(state_hgrn[:, 0], -1, -2), state_rwkv[:, 0], True, weights)
    y_prompt = yp.reshape(b_ctx, t_ctx, D_MODEL)
    y_sample = ys.reshape(b_lat, t_lat, D_MODEL)
    new_state_hgrn = jnp.swapaxes(s_h, -1, -2)[:, None]
    new_state_rwkv = s_r[:, None]
    return (y_prompt, y_sample, new_state_hgrn, new_state_rwkv)
```

```python
import functools

import numpy as np
import jax
import jax.numpy as jnp
from jax import lax
from jax.experimental import pallas as pl
from jax.experimental.pallas import tpu as pltpu

F32 = jnp.float32
BF16 = jnp.bfloat16

D_MODEL = 1024
GRID_W = 64
WIDTH_A = 512
HEAD_A = 128
N_HEADS_A = 4
WIDTH_B = 512
HEAD_B = 64
N_HEADS_B = 8
LORA_W = 32
LORA_A = 32
LORA_G = 96
D_FF = 2816
RMS_EPS = 1e-6
GN_EPS = 64e-5
DECAY_SCALE = 0.6065306597
P_A = 5 * WIDTH_A
P_B = 3 * WIDTH_B + 2 * LORA_W + LORA_A + LORA_G
LORA_COLS = 256
P_B_PAD = 3 * WIDTH_B + LORA_COLS

CHUNK = 64
LANES = 128
ROW_TILE = 256
VMEM_LIMIT = 56 * 1024 * 1024

NN = (((1,), (0,)), ((), ()))
NT = (((1,), (1,)), ((), ()))
TN = (((0,), (0,)), ((), ()))


def _dot(a, b, dims=NN):
    return lax.dot_general(a, b, dims, preferred_element_type=F32)


def _split(a):
    hi = a.astype(BF16)
    lo = (a - hi.astype(F32)).astype(BF16)
    return hi, lo


def _mm(a, b, dims=NN, passes=1):
    if passes == 1:
        return _dot(a.astype(BF16), b.astype(BF16), dims)
    ah, al = _split(a)
    bh, bl = _split(b)
    return _dot(ah, bh, dims) + (_dot(ah, bl, dims) + _dot(al, bh, dims))


def _mm_exact_lhs(m, x):
    n = x.shape[1]
    both = _dot(m, jnp.concatenate(_split(x), axis=1))
    return both[:, :n] + both[:, n:]


def _sigmoid(x):
    return 0.5 * jnp.tanh(0.5 * x) + 0.5


def _silu(x):
    return x * _sigmoid(x)


def _hgrn_level_consts(c, rev):
    t = np.arange(c)
    mats = [(t[None, :] <= t[:, None]).astype(np.float32)]
    masks = [np.eye(c, dtype=np.float32)]
    n = 1
    while n < c:
        blk, half = t // (2 * n), (t // n) % 2
        mid = blk * 2 * n + n
        m = np.zeros((c, c), np.float32)
        for row in range(c):
            if half[row] == 1:
                m[row, mid[row]:row + 1] = 1.0
            else:
                m[row, row + 1:mid[row]] = 1.0
        mats.append(m)
        masks.append(((blk[:, None] == blk[None, :]) & (half[:, None] == 1) & (half[None, :] == 0))
                     .astype(np.float32))
        n *= 2
    if rev:
        mats = [m[::-1, ::-1] for m in mats]
        masks = [m[::-1, ::-1] for m in masks]
    return np.concatenate(mats, 0), np.concatenate(masks, 0)


def _hgrn_consts(c):
    mf, kf = _hgrn_level_consts(c, False)
    mb, kb = _hgrn_level_consts(c, True)
    return np.stack([mf, mb]), np.stack([kf, kb])


def _rwkv_consts(c):
    t = np.arange(c)
    out = []
    for rev in (False, True):
        incl = (t[None, :] <= t[:, None]) if not rev else (t[None, :] >= t[:, None])
        strict = (t[None, :] < t[:, None]) if not rev else (t[None, :] > t[:, None])
        z = np.zeros((c, c), bool)
        cum = np.block([[incl, z], [z, z]])
        out.append(np.stack([cum, np.block([[strict, z], [z, strict]]), np.block([[incl, z], [z, incl]])]))
    return np.stack(out).astype(np.float32)


def _each(fn, *cols):
    return [fn(*args) for args in zip(*cols)]


def _interleave(*gens):
    results = [None] * len(gens)
    live = dict(enumerate(gens))
    while live:
        for i in list(live):
            try:
                next(live[i])
            except StopIteration as stop:
                results[i] = stop.value
                del live[i]
    return results


def _hgrn_chunks_local(units):
    q, k, v, g, m, masks, rev = (list(col) for col in zip(*units))
    c = q[0].shape[0]
    levels = m[0].shape[0] // c
    e_all = _each(_mm_exact_lhs, m, g)
    yield
    b = [x[:c] for x in e_all]
    edge = _each(lambda x, rv: x[0:1] if rv else x[c - 1:c], b, rev)
    kv = _each(lambda vv, kk, ed, bb: _mm(vv, kk * jnp.exp(ed - bb), TN), v, k, edge, b)
    yield
    sc = _each(lambda mk, a, kk: mk[:c] * _mm(a, kk, NT), masks, q, k)
    yield
    for l in range(1, levels):
        e = _each(lambda x: jnp.exp(x[l * c:(l + 1) * c]), e_all)
        sc = _each(lambda s, mk, a, kk, ee: s + mk[l * c:(l + 1) * c] * _mm(a * ee, kk * ee, NT), sc, masks, q, k, e)
        yield
    intra = _each(_mm, sc, v)
    yield
    return [(ii, qq * jnp.exp(bb), kvv, jnp.exp(ed)) for ii, qq, bb, kvv, ed in zip(intra, q, b, kv, edge)]


def _hgrn_chunk_state(st, part):
    intra, qb, kv, decay = part
    return intra + _mm(qb, st, NT), st * decay + kv


def _stack_heads(x, lane_head):
    return jnp.concatenate([jnp.where(lane_head == 0, x, 0.0), jnp.where(lane_head == 1, x, 0.0)], axis=0)


RWKV_PASSES_SCORE = 1
RWKV_PASSES_SOLVE = 1
RWKV_PASSES_STATE = 1
RWKV_LOCAL_PARTS = 7
RWKV_UNITS = 8
HGRN_GROUP = 4
MIXER_ROW_BLOCK = 256


def _rwkv_chunks_local(units):
    c = units[0][0].shape[0]
    lane_head = lax.broadcasted_iota(jnp.int32, (c, LANES), 1) // HEAD_B
    st = lambda x: _stack_heads(x, lane_head)
    r, k, v, kk, beta, lw, consts, rev = (list(col) for col in zip(*units))
    strict = [cs[1] for cs in consts]
    incl = [cs[2] for cs in consts]
    g = _each(lambda cs, x: _mm_exact_lhs(cs[0][:c, :c].astype(BF16), x), consts, lw)
    edge = _each(lambda x, rv: x[0:1] if rv else x[c - 1:c], g, rev)
    yield
    kg = _each(lambda x, gg, l: st(x * jnp.exp(gg - l)), kk, g, lw)
    rg = _each(lambda x, gg: st(x * jnp.exp(gg)), r, g)
    e_out = _each(lambda gg: jnp.exp(-gg), g)
    bi = _each(lambda x, e: st(x * e), beta, e_out)
    ki = _each(lambda x, e: st(x * e), k, e_out)
    vs = _each(st, v)
    a_all = _each(lambda a, b, cc, d: _mm(jnp.concatenate([a, b], 0), jnp.concatenate([cc, d], 0), NT, RWKV_PASSES_SCORE),
                  kg, rg, bi, ki)
    yield
    n = _each(lambda m, a: m * a[:2 * c, :2 * c], strict, a_all)
    a_ak = _each(lambda m, a: m * a[:2 * c, 2 * c:], strict, a_all)
    a_rb = _each(lambda m, a: m * a[2 * c:, :2 * c], incl, a_all)
    a_rk = _each(lambda m, a: m * a[2 * c:, 2 * c:], incl, a_all)
    av = _each(lambda a, b, x: _mm(jnp.concatenate([a, b], 0), x, passes=RWKV_PASSES_SCORE), a_ak, a_rk, vs)
    yield
    eye = (lax.broadcasted_iota(jnp.int32, (2 * c, 2 * c), 0) == lax.broadcasted_iota(jnp.int32, (2 * c, 2 * c), 1)).astype(F32)
    tinv = _each(lambda a: eye - a, n)
    power = _each(lambda a: _mm(a, a, passes=RWKV_PASSES_SOLVE), n)
    yield
    span = 4
    while span < c:
        both = _each(lambda p, t: _mm(p, jnp.concatenate([p, t], axis=1), passes=RWKV_PASSES_SOLVE), power, tinv)
        power = [b[:, :2 * c] for b in both]
        tinv = _each(lambda t, b: t + b[:, 2 * c:], tinv, both)
        span *= 2
        yield
    tinv = _each(lambda p, t: t + _mm(p, t, passes=RWKV_PASSES_SOLVE), power, tinv)
    yield
    x = _each(lambda t, a, b: _mm(t, jnp.concatenate([a, b[:2 * c]], axis=1), passes=RWKV_PASSES_SOLVE), tinv, kg, av)
    yield
    e_edge = _each(lambda ed, gg: jnp.exp(ed - gg), edge, g)
    kv = _each(lambda a, b, e: _mm(a, st(b * e), TN, RWKV_PASSES_STATE), vs, k, e_edge)
    yield
    bgc = _each(lambda b, e: st(b * e), beta, e_edge)
    return [((xx[:, :LANES], rr, xx[:, LANES:], ab, a[2 * c:], bg, kvv), jnp.exp(ed))
            for xx, rr, ab, a, bg, kvv, ed in zip(x, rg, a_rb, av, bgc, kv, edge)]


def _rwkv_chunks_state(states, parts, decays):
    w1, rg, x2, a_rb, y0, bgc, kv = (list(col) for col in zip(*parts))
    c2 = w1[0].shape[0]
    hs = _each(lambda a, b, s: _mm(jnp.concatenate([a, b], 0), s, NT, RWKV_PASSES_STATE), w1, rg, states)
    yield
    u = _each(lambda h, x: -(h[:c2] + x), hs, x2)
    s = _each(lambda s_, d, uu, b, kv_: s_ * d + _mm(uu, b, TN, RWKV_PASSES_STATE) + kv_, states, decays, u, bgc, kv)
    yield
    y = _each(lambda h, y_, a, uu: h[c2:] + y_ + _mm(a, uu, passes=RWKV_PASSES_STATE), hs, y0, a_rb, u)
    yield
    return [yy[:c2 // 2] + yy[c2 // 2:] for yy in y], s


def _params(*semantics):
    return pltpu.CompilerParams(dimension_semantics=semantics, vmem_limit_bytes=VMEM_LIMIT)


def _full(shape):
    return pl.BlockSpec(shape, lambda *_: (0,) * len(shape))


def _modulated_norm(x, norm_w, scale, shift):
    y = x * lax.rsqrt(jnp.mean(x * x, axis=-1, keepdims=True) + RMS_EPS)
    return (y * norm_w) * (1.0 + scale) + shift


def _mod_kernel(c_ref, w_ref, b_ref, o_ref):
    o_ref[...] = _mm(_silu(c_ref[...]), w_ref[...], passes=3) + b_ref[...]


def _modulation(cvec, ada_w, ada_b):
    n = ada_w.shape[1]
    tn = n // 4
    return pl.pallas_call(
        _mod_kernel,
        grid=(n // tn,),
        in_specs=[_full(cvec.shape), pl.BlockSpec((D_MODEL, tn), lambda j: (0, j)), pl.BlockSpec((1, tn), lambda j: (0, j))],
        out_specs=pl.BlockSpec((cvec.shape[0], tn), lambda j: (0, j)),
        out_shape=jax.ShapeDtypeStruct((cvec.shape[0], n), F32),
        compiler_params=_params("arbitrary"),
        name="modulation",
    )(cvec, ada_w, ada_b)


def _in_proj_kernel(x_ref, xp_ref, xn_ref, mod_ref, nw_ref, wa_ref, wb_ref, cw_ref, ua_ref, ub_ref, *, seq_tiles):
    i = pl.program_id(0)
    mod = mod_ref[...]
    shift, scale = mod[:, 0:D_MODEL], mod[:, D_MODEL:2 * D_MODEL]
    nw = nw_ref[...]
    h = _modulated_norm(x_ref[...], nw, scale, shift).astype(BF16)
    ua_ref[...] = _dot(h, wa_ref[...])
    ub = _dot(h, wb_ref[...])
    halo = jnp.concatenate([xp_ref[...], xn_ref[...]], axis=0)
    hh = _modulated_norm(halo, nw, scale, shift).astype(BF16)
    ubh = _dot(hh, wb_ref[...])
    first = (i % seq_tiles) == 0
    last = (i % seq_tiles) == seq_tiles - 1
    prev_row = jnp.where(first, 0.0, ubh[7:8])
    next_row = jnp.where(last, 0.0, ubh[8:9])
    rows = lax.broadcasted_iota(jnp.int32, ub.shape, 0)
    n = ub.shape[0]
    below = jnp.where(rows == 0, prev_row, pltpu.roll(ub, 1, 0))
    above = jnp.where(rows == n - 1, next_row, pltpu.roll(ub, n - 1, 0))
    cw = cw_ref[...]
    ub_ref[...] = cw[0:1] * below + cw[1:2] * ub + cw[2:3] * above


def _in_proj(x, mod, norm_w, w_a, w_b, conv_w, seq_tiles, mod_of_tile):
    t = x.shape[0]
    nt = t // ROW_TILE
    sub = ROW_TILE // 8
    return pl.pallas_call(
        functools.partial(_in_proj_kernel, seq_tiles=seq_tiles),
        grid=(nt,),
        in_specs=[
            pl.BlockSpec((ROW_TILE, D_MODEL), lambda i: (i, 0)),
            pl.BlockSpec((8, D_MODEL), lambda i: (jnp.maximum(i * sub - 1, 0), 0)),
            pl.BlockSpec((8, D_MODEL), lambda i: (jnp.minimum((i + 1) * sub, nt * sub - 1), 0)),
            pl.BlockSpec((None, 1, 6 * D_MODEL), lambda i: (mod_of_tile(i), 0, 0)),
            _full((1, D_MODEL)), _full(w_a.shape), _full(w_b.shape), _full(conv_w.shape),
        ],
        out_specs=[pl.BlockSpec((ROW_TILE, P_A), lambda i: (i, 0)), pl.BlockSpec((ROW_TILE, P_B_PAD), lambda i: (i, 0))],
        out_shape=[jax.ShapeDtypeStruct((t, P_A), F32), jax.ShapeDtypeStruct((t, P_B_PAD), F32)],
        compiler_params=_params("parallel"),
        name="in_proj",
    )(x, x, x, mod, norm_w, w_a, w_b, conv_w)


def _hgrn_kernel(q_ref, i_ref, zf_ref, zb_ref, g_ref, lb_ref, nw_ref, s0_ref, m_ref, mask_ref,
                 o_ref, sout_ref, acc_s, st_ref, *, nc, group, block):
    c = CHUNK
    p0, p1 = lb_ref[0], lb_ref[1]
    mx = jnp.maximum(p0, p1)
    e0, e1 = jnp.exp(p0 - mx), jnp.exp(p1 - mx)
    lb = e0 / (e0 + e1)
    st_ref[...] = s0_ref[...]

    def rows_of(ci):
        return pl.ds(pl.multiple_of(ci * c, c), c)

    def chunk_of(gi, j, d):
        cf = gi * group + j
        return cf if d == 0 else nc - 1 - cf

    def scan_group(gi, carry):
        units = []
        for j in range(group):
            for d in range(2):
                rows = rows_of(chunk_of(gi, j, d))
                lo = lb[d:d + 1]
                f = lo + (1.0 - lo) * _sigmoid((zf_ref, zb_ref)[d][rows, :])
                units.append((_silu(q_ref[rows, :]), 1.0 - f, i_ref[rows, :], jnp.log(f), m_ref[d], mask_ref[d], d == 1))
        parts, = _interleave(_hgrn_chunks_local(units))
        st = [st_ref[0], st_ref[1]]
        for j in range(group):
            for d in range(2):
                o, st[d] = _hgrn_chunk_state(st[d], parts[2 * j + d])
                acc_s[rows_of(chunk_of(gi, j, d)), :] += o
        st_ref[0], st_ref[1] = st
        return carry

    acc_s[...] = jnp.zeros(acc_s.shape, F32)
    lax.fori_loop(0, nc // group, scan_group, 0)

    def finish(bi, carry):
        rows = pl.ds(pl.multiple_of(bi * block, block), block)
        o = acc_s[rows, :]
        o = o * lax.rsqrt(jnp.mean(o * o, axis=-1, keepdims=True) + RMS_EPS) * nw_ref[...]
        o_ref[rows, :] = o * _silu(g_ref[rows, :])
        return carry

    lax.fori_loop(0, nc * c // block, finish, 0, unroll=2 if nc * c // block % 2 == 0 else 1)
    sout_ref[...] = st_ref[...]


def _hgrn_mixer(u_a, lb_raw, norm_w, s0t, seq_len):
    t = u_a.shape[0]
    nb = t // seq_len
    nc = seq_len // CHUNK
    group = min(HGRN_GROUP, nc)
    assert nc % group == 0
    mats, masks = _hgrn_consts(CHUNK)
    mats = jnp.asarray(mats, BF16)
    masks = jnp.asarray(masks, F32)
    col = lambda part: pl.BlockSpec((seq_len, HEAD_A), lambda b, h: (b, part * N_HEADS_A + h))
    state_spec = pl.BlockSpec((None, 2, None, HEAD_A, HEAD_A), lambda b, h: (b, 0, h, 0, 0))
    return pl.pallas_call(
        functools.partial(_hgrn_kernel, nc=nc, group=group, block=min(MIXER_ROW_BLOCK, seq_len)),
        grid=(nb, N_HEADS_A),
        in_specs=[col(0), col(1), col(2), col(3), col(4),
                  pl.BlockSpec((2, 2, HEAD_A), lambda b, h: (0, 0, h)),
                  pl.BlockSpec((1, HEAD_A), lambda b, h: (0, h)),
                  state_spec, _full(mats.shape), _full(masks.shape)],
        out_specs=[pl.BlockSpec((seq_len, HEAD_A), lambda b, h: (b, h)), state_spec],
        out_shape=[jax.ShapeDtypeStruct((t, WIDTH_A), F32), jax.ShapeDtypeStruct((nb, 2, N_HEADS_A, HEAD_A, HEAD_A), F32)],
        scratch_shapes=[pltpu.VMEM((seq_len, HEAD_A), F32), pltpu.VMEM((2, HEAD_A, HEAD_A), F32)],
        compiler_params=_params("parallel", "parallel"),
        name="hgrn2_scan",
    )(u_a, u_a, u_a, u_a, u_a, lb_raw, norm_w, s0t, mats, masks)


def _mm_exact_rhs(x, m):
    n = x.shape[0]
    both = _dot(jnp.concatenate(_split(x), axis=0), m)
    return both[:n] + both[n:]


def _rwkv_kernel(r_ref, k_ref, v_ref, lora_ref, w0_ref, a0_ref, kkw_ref, ka_ref, rk_ref, lnw_ref, lnb_ref,
                 wdec_ref, wa_ref, wg_ref, s0_ref, consts_ref, seg_ref,
                 o_ref, sout_ref, lwf_s, lwb_s, kk_s, beta_s, kmod_s, y_s, loca_s, deca_s, locb_s, decb_s, st_ref,
                 *, ns, nc, group, block):
    c = CHUNK
    seq_len = nc * c
    chains = [(s, d) for s in range(ns) for d in range(2)]
    seg = seg_ref[...]

    for ch, (s, d) in enumerate(chains):
        st_ref[ch] = jnp.zeros((LANES, LANES), F32)
        st_ref[ch, 0:HEAD_B, 0:HEAD_B] = s0_ref[s, d, 0]
        st_ref[ch, HEAD_B:LANES, HEAD_B:LANES] = s0_ref[s, d, 1]

    def block_of(bi):
        return pl.ds(pl.multiple_of(bi * block, block), block)

    def prepare(bi, carry):
        rows = block_of(bi)
        lora = lora_ref[rows, :]
        th, tl = _split(jnp.tanh(lora))
        w_hi, w_lo = wdec_ref[0], wdec_ref[1]
        dec = _dot(th, w_hi) + (_dot(th, w_lo) + _dot(tl, w_hi))
        w0 = w0_ref[...]
        lwf_s[rows, :] = -DECAY_SCALE * _sigmoid(w0[0:1] + dec[:, :LANES])
        lwb_s[rows, :] = -DECAY_SCALE * _sigmoid(w0[1:2] + dec[:, LANES:])
        a = _sigmoid(a0_ref[...] + _mm(lora, wa_ref[...]))
        k = k_ref[rows, :]
        kk = k * kkw_ref[...]
        kk = kk * lax.rsqrt(_mm(kk * kk, seg) + 1e-12)
        kk_s[rows, :] = kk
        beta_s[rows, :] = kk * a
        kmod_s[rows, :] = k * (1.0 + (a - 1.0) * ka_ref[...])
        return carry

    lax.fori_loop(0, ns * seq_len // block, prepare, 0, unroll=2)

    def rows_of(gi, j, s, d):
        cf = gi * group + j
        ci = cf if d == 0 else nc - 1 - cf
        return pl.ds(pl.multiple_of(s * seq_len + ci * c, c), c)

    def local_stage(gi, loc, dec):
        units = []
        for j in range(group):
            for s, d in chains:
                rows = rows_of(gi, j, s, d)
                units.append((r_ref[rows, :], kmod_s[rows, :], v_ref[rows, :], kk_s[rows, :], beta_s[rows, :],
                              (lwf_s, lwb_s)[d][rows, :], consts_ref[d], d == 1))
        results = yield from _rwkv_chunks_local(units)
        for u, (parts, decay) in enumerate(results):
            j, ch = divmod(u, len(chains))
            for idx, part in enumerate(parts):
                loc[ch, j, idx] = part
            dec[ch, j] = decay

    def state_stage(gi, loc, dec):
        states = [st_ref[ch] for ch in range(len(chains))]
        for j in range(group):
            parts = [[loc[ch, j, idx] for idx in range(RWKV_LOCAL_PARTS)] for ch in range(len(chains))]
            ys, states = yield from _rwkv_chunks_state(states, parts, [dec[ch, j] for ch in range(len(chains))])
            for y, (s, d) in zip(ys, chains):
                y_s[rows_of(gi, j, s, d), :] += y
        for ch, state in enumerate(states):
            st_ref[ch] = state

    y_s[...] = jnp.zeros(y_s.shape, F32)
    ng = nc // group
    _interleave(local_stage(0, loca_s, deca_s))

    def two_groups(p, carry):
        _interleave(local_stage(2 * p + 1, locb_s, decb_s), state_stage(2 * p, loca_s, deca_s))
        _interleave(local_stage(2 * p + 2, loca_s, deca_s), state_stage(2 * p + 1, locb_s, decb_s))
        return carry

    lax.fori_loop(0, ng // 2 - 1, two_groups, 0)
    _interleave(local_stage(ng - 1, locb_s, decb_s), state_stage(ng - 2, loca_s, deca_s))
    _interleave(state_stage(ng - 1, locb_s, decb_s))

    def finish(bi, carry):
        rows = block_of(bi)
        y = y_s[rows, :]
        inv_n = 1.0 / HEAD_B
        mu = _mm_exact_rhs(y, seg) * inv_n
        dy = y - mu
        var = _mm(dy * dy, seg) * inv_n
        yn = dy * lax.rsqrt(var + GN_EPS) * lnw_ref[...] + lnb_ref[...]
        v = v_ref[rows, :]
        bonus = _mm(r_ref[rows, :] * kmod_s[rows, :] * rk_ref[...], seg) * v
        g = _mm(_sigmoid(lora_ref[rows, :]), wg_ref[...])
        o_ref[rows, :] = (yn + bonus) * g
        return carry

    lax.fori_loop(0, ns * seq_len // block, finish, 0, unroll=2)
    for ch, (s, d) in enumerate(chains):
        sout_ref[s, d, 0] = st_ref[ch, 0:HEAD_B, 0:HEAD_B]
        sout_ref[s, d, 1] = st_ref[ch, HEAD_B:LANES, HEAD_B:LANES]


def _rwkv_mixer(u_b, p, s0, seq_len):
    t = u_b.shape[0]
    nb = t // seq_len
    nc = seq_len // CHUNK
    group = max(1, min(RWKV_UNITS // 2, nc // 4))
    ns = max(1, min(RWKV_UNITS // (2 * group), nb))
    assert nc % (2 * group) == 0 and nb % ns == 0
    rows = ns * seq_len
    block = min(MIXER_ROW_BLOCK, rows)
    pairs = N_HEADS_B // 2
    consts = jnp.asarray(_rwkv_consts(CHUNK), F32)
    lane = np.arange(LANES) // HEAD_B
    seg = jnp.asarray(lane[:, None] == lane[None, :], BF16)
    col = lambda part: pl.BlockSpec((rows, LANES), lambda b, h: (b, part * pairs + h))
    vec = lambda n: pl.BlockSpec((n, LANES), lambda b, h: (0, h))
    state_spec = pl.BlockSpec((ns, 2, 2, HEAD_B, HEAD_B), lambda b, h: (b, 0, h, 0, 0))
    scr = lambda: pltpu.VMEM((rows, LANES), F32)
    loc = lambda: pltpu.VMEM((2 * ns, group, RWKV_LOCAL_PARTS, LANES, LANES), F32)
    dec = lambda: pltpu.VMEM((2 * ns, group, 1, LANES), F32)
    return pl.pallas_call(
        functools.partial(_rwkv_kernel, ns=ns, nc=nc, group=group, block=block),
        grid=(nb // ns, pairs),
        in_specs=[col(0), col(1), col(2),
                  pl.BlockSpec((rows, LORA_COLS), lambda b, h: (b, 3 * WIDTH_B // LORA_COLS)),
                  vec(2), vec(1), vec(1), vec(1), vec(1), vec(1), vec(1),
                  pl.BlockSpec((None, 2, LORA_COLS, 2 * LANES), lambda b, h: (h, 0, 0, 0)),
                  pl.BlockSpec((LORA_COLS, LANES), lambda b, h: (0, h)),
                  pl.BlockSpec((LORA_COLS, LANES), lambda b, h: (0, h)),
                  state_spec, _full(consts.shape), _full(seg.shape)],
        out_specs=[pl.BlockSpec((rows, LANES), lambda b, h: (b, h)), state_spec],
        out_shape=[jax.ShapeDtypeStruct((t, WIDTH_B), F32), jax.ShapeDtypeStruct((nb, 2, N_HEADS_B, HEAD_B, HEAD_B), F32)],
        scratch_shapes=[scr(), scr(), scr(), scr(), scr(), scr(), loc(), dec(), loc(), dec(),
                        pltpu.VMEM((2 * ns, LANES, LANES), F32)],
        compiler_params=_params("parallel", "parallel"),
        name="rwkv7_scan",
    )(u_b, u_b, u_b, u_b, p["w0"], p["a0"], p["k_k"], p["k_a"], p["r_k"], p["ln_w"], p["ln_b"],
      p["w_dec"], p["w_a"], p["w_g"], s0, consts, seg)


def _mix_out_kernel(oa_ref, ob_ref, x_ref, mod_ref, woa_ref, wob_ref, nw_ref, wg_ref, wu_ref, x1_ref, g_ref, u_ref):
    mod = mod_ref[...]
    gate1 = mod[:, 2 * D_MODEL:3 * D_MODEL]
    shift2, scale2 = mod[:, 3 * D_MODEL:4 * D_MODEL], mod[:, 4 * D_MODEL:5 * D_MODEL]
    mix = _dot(oa_ref[...].astype(BF16), woa_ref[...]) + _dot(ob_ref[...].astype(BF16), wob_ref[...])
    x1 = x_ref[...] + gate1 * mix
    x1_ref[...] = x1
    h = _modulated_norm(x1, nw_ref[...], scale2, shift2).astype(BF16)
    g_ref[...] = _dot(h, wg_ref[...])
    u_ref[...] = _dot(h, wu_ref[...])


def _mix_out(o_a, o_b, x, mod, w_out_a, w_out_b, norm_w, w_gate, w_up, mod_of_tile):
    t = x.shape[0]
    row = lambda n: pl.BlockSpec((ROW_TILE, n), lambda i: (i, 0))
    return pl.pallas_call(
        _mix_out_kernel,
        grid=(t // ROW_TILE,),
        in_specs=[row(WIDTH_A), row(WIDTH_B), row(D_MODEL),
                  pl.BlockSpec((None, 1, 6 * D_MODEL), lambda i: (mod_of_tile(i), 0, 0)),
                  _full(w_out_a.shape), _full(w_out_b.shape), _full((1, D_MODEL)), _full(w_gate.shape), _full(w_up.shape)],
        out_specs=[row(D_MODEL), row(D_FF), row(D_FF)],
        out_shape=[jax.ShapeDtypeStruct((t, D_MODEL), F32), jax.ShapeDtypeStruct((t, D_FF), F32),
                   jax.ShapeDtypeStruct((t, D_FF), F32)],
        compiler_params=_params("parallel"),
        name="mix_out_ffn_in",
    )(o_a, o_b, x, mod, w_out_a, w_out_b, norm_w, w_gate, w_up)


def _gelu_tanh(x):
    half = 0.5 * x
    return half + half * jnp.tanh(x * (0.7978845608028654 + (0.7978845608028654 * 0.044715) * (x * x)))


def _ffn_out_kernel(g_ref, ga_ref, gb_ref, u_ref, x1_ref, mod_ref, cw_ref, cb_ref, wd_ref, fw_ref, y_ref,
                    *, seq_tiles, grid_conv):
    i = pl.program_id(0)
    cw = cw_ref[...]
    g = g_ref[...]
    n = g.shape[0]
    if grid_conv:
        top = (i % seq_tiles) == 0
        bottom = (i % seq_tiles) == seq_tiles - 1
        ext = jnp.concatenate([jnp.where(top, 0.0, ga_ref[...]), g, jnp.where(bottom, 0.0, gb_ref[...])], axis=0)
        width = GRID_W
        taps = (0, 1, 2)
    else:
        ext = g
        width = n
        taps = (1,)
    sums = []
    for dc in range(3):
        acc = None
        for dr in taps:
            off = (GRID_W * dr) if grid_conv else 0
            term = cw[3 * dr + dc:3 * dr + dc + 1] * ext[off:off + n]
            acc = term if acc is None else acc + term
        sums.append(acc)
    col = lax.broadcasted_iota(jnp.int32, g.shape, 0) % width
    gt = (sums[1] + cb_ref[...]
          + jnp.where(col == 0, 0.0, pltpu.roll(sums[0], 1, 0))
          + jnp.where(col == width - 1, 0.0, pltpu.roll(sums[2], n - 1, 0)))
    act = (_gelu_tanh(gt) * u_ref[...]).astype(BF16)
    mod = mod_ref[...]
    gate2 = mod[:, 5 * D_MODEL:6 * D_MODEL]
    x2 = x1_ref[...] + gate2 * _dot(act, wd_ref[...])
    y_ref[...] = x2 * lax.rsqrt(jnp.mean(x2 * x2, axis=-1, keepdims=True) + RMS_EPS) * fw_ref[...]


def _ffn_out(gpre, up, x1, mod, conv_w, conv_b, w_down, final_w, seq_tiles, grid_conv, mod_of_tile):
    t = x1.shape[0]
    nt = t // ROW_TILE
    per = ROW_TILE // GRID_W
    row = lambda n: pl.BlockSpec((ROW_TILE, n), lambda i: (i, 0))
    return pl.pallas_call(
        functools.partial(_ffn_out_kernel, seq_tiles=seq_tiles, grid_conv=grid_conv),
        grid=(nt,),
        in_specs=[row(D_FF),
                  pl.BlockSpec((GRID_W, D_FF), lambda i: (jnp.maximum(i * per - 1, 0), 0)),
                  pl.BlockSpec((GRID_W, D_FF), lambda i: (jnp.minimum((i + 1) * per, nt * per - 1), 0)),
                  row(D_FF), row(D_MODEL),
                  pl.BlockSpec((None, 1, 6 * D_MODEL), lambda i: (mod_of_tile(i), 0, 0)),
                  _full(conv_w.shape), _full((1, D_FF)), _full(w_down.shape), _full((1, D_MODEL))],
        out_specs=row(D_MODEL),
        out_shape=jax.ShapeDtypeStruct((t, D_MODEL), F32),
        compiler_params=_params("parallel"),
        name="ffn_out",
    )(gpre, gpre, gpre, up, x1, mod, conv_w, conv_b, w_down, final_w)


def _block(x, seq_len, mod, mod_of_tile, s_hgrn_t, s_rwkv, grid_conv, w):
    seq_tiles = seq_len // ROW_TILE
    u_a, u_b = _in_proj(x, mod, w["norm_mix_w"], w["w_in_a"], w["w_in_b"], w["rwkv_conv"], seq_tiles, mod_of_tile)
    o_a, s_h = _hgrn_mixer(u_a, w["hgrn_lb"], w["hgrn_norm_w"], s_hgrn_t, seq_len)
    o_b, s_r = _rwkv_mixer(u_b, w["rwkv"], s_rwkv, seq_len)
    x1, gpre, up = _mix_out(o_a, o_b, x, mod, w["w_out_a"], w["w_out_b"], w["norm_ffn_w"], w["ffn_w_gate"],
                            w["ffn_w_up"], mod_of_tile)
    y = _ffn_out(gpre, up, x1, mod, w["ffn_conv"], w["ffn_conv_b"], w["ffn_w_down"], w["final_norm_w"],
                 seq_tiles, grid_conv, mod_of_tile)
    return y, s_h, s_r


def _place_rows(w, start, total):
    return jnp.zeros((total, w.shape[1]), w.dtype).at[start:start + w.shape[0]].set(w)


def kernel(x_prompt, x_sample, state_hgrn, state_rwkv, c, c_ctx, ada_w, ada_b, norm_mix_w, w_in, hgrn_lb, hgrn_norm_w, rwkv_conv, rwkv_w0, rwkv_w2, rwkv_a0, rwkv_a2, rwkv_g2, rwkv_k_k, rwkv_k_a, rwkv_r_k, rwkv_ln_w, rwkv_ln_b, w_out, norm_ffn_w, ffn_w_gate, ffn_w_up, ffn_conv, ffn_conv_b, ffn_w_down, final_norm_w):
    assert w_in.shape[0] == 1, "one trunk layer"
    b_ctx, t_ctx, _ = x_prompt.shape
    b_lat, t_lat, _ = x_sample.shape
    row = lambda v: v.reshape(1, -1)
    pad_cols = P_B_PAD - P_B
    w_dec = jnp.concatenate([_place_rows(rwkv_w2[0, 0], 0, LORA_COLS), _place_rows(rwkv_w2[0, 1], LORA_W, LORA_COLS)], axis=1)
    pairs = N_HEADS_B // 2
    w_dec = w_dec.reshape(LORA_COLS, 2, pairs, LANES).transpose(2, 0, 1, 3).reshape(pairs, LORA_COLS, 2 * LANES)
    w_dec = jnp.stack(_split(w_dec), axis=1)
    weights = dict(
        norm_mix_w=row(norm_mix_w[0]),
        w_in_a=w_in[0, :, :P_A].astype(BF16),
        w_in_b=jnp.pad(w_in[0, :, P_A:], ((0, 0), (0, pad_cols))).astype(BF16),
        rwkv_conv=jnp.pad(rwkv_conv[0], ((0, 0), (0, pad_cols))),
        hgrn_lb=hgrn_lb,
        hgrn_norm_w=row(hgrn_norm_w[0]),
        rwkv=dict(w0=rwkv_w0[0], a0=row(rwkv_a0[0]), k_k=row(rwkv_k_k[0]), k_a=row(rwkv_k_a[0]), r_k=row(rwkv_r_k[0]),
                  ln_w=row(rwkv_ln_w[0]), ln_b=row(rwkv_ln_b[0]), w_dec=w_dec,
                  w_a=_place_rows(rwkv_a2[0], 2 * LORA_W, LORA_COLS),
                  w_g=_place_rows(rwkv_g2[0], 2 * LORA_W + LORA_A, LORA_COLS)),
        w_out_a=w_out[0, :WIDTH_A].astype(BF16),
        w_out_b=w_out[0, WIDTH_A:].astype(BF16),
        norm_ffn_w=row(norm_ffn_w[0]),
        ffn_w_gate=ffn_w_gate[0].astype(BF16),
        ffn_w_up=ffn_w_up[0].astype(BF16),
        ffn_conv=ffn_conv[0].reshape(9, D_FF),
        ffn_conv_b=row(ffn_conv_b[0]),
        ffn_w_down=ffn_w_down[0].astype(BF16),
        final_norm_w=row(final_norm_w),
    )
    cvec = jnp.concatenate([c_ctx[None, :], c, jnp.zeros((8 - 1 - b_lat, D_MODEL), F32)], axis=0)
    mod = _modulation(cvec, ada_w[0], row(ada_b[0])).reshape(8, 1, 6 * D_MODEL)

    zeros_h = jnp.zeros((b_ctx, 2, N_HEADS_A, HEAD_A, HEAD_A), F32)
    zeros_r = jnp.zeros((b_ctx, 2, N_HEADS_B, HEAD_B, HEAD_B), F32)
    yp, s_h, s_r = _block(x_prompt.reshape(b_ctx * t_ctx, D_MODEL), t_ctx, mod, lambda i: 0, zeros_h, zeros_r, False, weights)
    lat_tiles = t_lat // ROW_TILE
    ys, _, _ = _block(x_sample.reshape(b_lat * t_lat, D_MODEL), t_lat, mod, lambda i: 1 + i // lat_tiles,
                      jnp.swapaxes(state_hgrn[:, 0], -1, -2), state_rwkv[:, 0], True, weights)
    y_prompt = yp.reshape(b_ctx, t_ctx, D_MODEL)
    y_sample = ys.reshape(b_lat, t_lat, D_MODEL)
    new_state_hgrn = jnp.swapaxes(s_h, -1, -2)[:, None]
    new_state_rwkv = s_r[:, None]
    return (y_prompt, y_sample, new_state_hgrn, new_state_rwkv)
```

```python
import functools

import numpy as np
import jax
import jax.numpy as jnp
from jax import lax
from jax.experimental import pallas as pl
from jax.experimental.pallas import tpu as pltpu

F32 = jnp.float32
BF16 = jnp.bfloat16

D_MODEL = 1024
GRID_W = 64
WIDTH_A = 512
HEAD_A = 128
N_HEADS_A = 4
WIDTH_B = 512
HEAD_B = 64
N_HEADS_B = 8
LORA_W = 32
LORA_A = 32
LORA_G = 96
D_FF = 2816
RMS_EPS = 1e-6
GN_EPS = 64e-5
DECAY_SCALE = 0.6065306597
P_A = 5 * WIDTH_A
P_B = 3 * WIDTH_B + 2 * LORA_W + LORA_A + LORA_G
LORA_COLS = 256
P_B_PAD = 3 * WIDTH_B + LORA_COLS

CHUNK = 64
LANES = 128
SUBLANES = 8
ROW_TILE = 256
FFN_COL_BLOCK = 256
VMEM_LIMIT = 56 * 1024 * 1024

NN = (((1,), (0,)), ((), ()))
NT = (((1,), (1,)), ((), ()))
TN = (((0,), (0,)), ((), ()))


def _dot(a, b, dims=NN):
    return lax.dot_general(a, b, dims, preferred_element_type=F32)


def _split(a):
    hi = a.astype(BF16)
    lo = (a - hi.astype(F32)).astype(BF16)
    return hi, lo


def _mm(a, b, dims=NN, passes=1):
    if passes == 1:
        return _dot(a.astype(BF16), b.astype(BF16), dims)
    ah, al = _split(a)
    bh, bl = _split(b)
    return _dot(ah, bh, dims) + (_dot(ah, bl, dims) + _dot(al, bh, dims))


def _mm_exact_lhs(m, x):
    n = x.shape[1]
    both = _dot(m, jnp.concatenate(_split(x), axis=1))
    return both[:, :n] + both[:, n:]


def _sigmoid(x):
    return 0.5 * jnp.tanh(0.5 * x) + 0.5


def _silu(x):
    return x * _sigmoid(x)


def _hgrn_level_consts(c, rev):
    t = np.arange(c)
    mats = [(t[None, :] <= t[:, None]).astype(np.float32)]
    masks = [np.eye(c, dtype=np.float32)]
    n = 1
    while n < c:
        blk, half = t // (2 * n), (t // n) % 2
        mid = blk * 2 * n + n
        m = np.zeros((c, c), np.float32)
        for row in range(c):
            if half[row] == 1:
                m[row, mid[row]:row + 1] = 1.0
            else:
                m[row, row + 1:mid[row]] = 1.0
        mats.append(m)
        masks.append(((blk[:, None] == blk[None, :]) & (half[:, None] == 1) & (half[None, :] == 0))
                     .astype(np.float32))
        n *= 2
    if rev:
        mats = [m[::-1, ::-1] for m in mats]
        masks = [m[::-1, ::-1] for m in masks]
    return np.concatenate(mats, 0), np.concatenate(masks, 0)


def _hgrn_consts(c):
    mf, kf = _hgrn_level_consts(c, False)
    mb, kb = _hgrn_level_consts(c, True)
    mat_levels = 1 + int(np.log2(SUBLANES))
    return np.stack([mf, mb])[:, :mat_levels * c], np.stack([kf, kb])


def _rwkv_consts(c):
    t = np.arange(c)
    out = []
    for rev in (False, True):
        incl = (t[None, :] <= t[:, None]) if not rev else (t[None, :] >= t[:, None])
        strict = (t[None, :] < t[:, None]) if not rev else (t[None, :] > t[:, None])
        z = np.zeros((c, c), bool)
        cum = np.block([[incl, z], [z, z]])
        out.append(np.stack([cum, np.block([[strict, z], [z, strict]]), np.block([[incl, z], [z, incl]])]))
    return np.stack(out).astype(np.float32)


def _each(fn, *cols):
    return [fn(*args) for args in zip(*cols)]


def _interleave(*gens):
    results = [None] * len(gens)
    live = dict(enumerate(gens))
    while live:
        for i in list(live):
            try:
                next(live[i])
            except StopIteration as stop:
                results[i] = stop.value
                del live[i]
    return results


def _pair_exponent(b, n, rev):
    parts = []
    for base in range(0, b.shape[0], 2 * n):
        first, second = b[base:base + n], b[base + n:base + 2 * n]
        if rev:
            r = jnp.broadcast_to(b[base + n:base + n + 1], first.shape)
            parts += [first - r, r - second]
        else:
            r = jnp.broadcast_to(b[base + n - 1:base + n], first.shape)
            parts += [r - first, second - r]
    return jnp.concatenate(parts, axis=0)


def _hgrn_chunks_local(units):
    q, k, v, g, m, masks, rev = (list(col) for col in zip(*units))
    c = q[0].shape[0]
    levels = masks[0].shape[0] // c
    mat_levels = m[0].shape[0] // c
    e_all = _each(_mm_exact_lhs, m, g)
    yield
    b = [x[:c] for x in e_all]
    edge = _each(lambda x, rv: x[0:1] if rv else x[c - 1:c], b, rev)
    kv = _each(lambda vv, kk, ed, bb: _mm(vv, kk * jnp.exp(ed - bb), TN), v, k, edge, b)
    yield
    sc = _each(lambda mk, a, kk: mk[:c] * _mm(a, kk, NT), masks, q, k)
    yield
    for l in range(1, levels):
        if l < mat_levels:
            e = _each(lambda x: jnp.exp(x[l * c:(l + 1) * c]), e_all)
        else:
            e = _each(lambda bb, rv: jnp.exp(_pair_exponent(bb, 1 << (l - 1), rv)), b, rev)
        sc = _each(lambda s, mk, a, kk, ee: s + mk[l * c:(l + 1) * c] * _mm(a * ee, kk * ee, NT), sc, masks, q, k, e)
        yield
    intra = _each(_mm, sc, v)
    yield
    return [(ii, qq * jnp.exp(bb), kvv, jnp.exp(ed)) for ii, qq, bb, kvv, ed in zip(intra, q, b, kv, edge)]


def _hgrn_chunk_state(st, part):
    intra, qb, kv, decay = part
    return intra + _mm(qb, st, NT), st * decay + kv


def _stack_heads(x, lane_head):
    return jnp.concatenate([jnp.where(lane_head == 0, x, 0.0), jnp.where(lane_head == 1, x, 0.0)], axis=0)


RWKV_PASSES_SCORE = 1
RWKV_PASSES_SOLVE = 1
RWKV_PASSES_STATE = 1
RWKV_LOCAL_PARTS = 7
RWKV_UNITS = 8
HGRN_UNITS = 16
MIXER_ROW_BLOCK = 256


def _rwkv_chunks_local(units):
    c = units[0][0].shape[0]
    lane_head = lax.broadcasted_iota(jnp.int32, (c, LANES), 1) // HEAD_B
    st = lambda x: _stack_heads(x, lane_head)
    r, k, v, kk, beta, lw, consts, rev = (list(col) for col in zip(*units))
    strict = [cs[1] for cs in consts]
    incl = [cs[2] for cs in consts]
    g = _each(lambda cs, x: _mm_exact_lhs(cs[0][:c, :c].astype(BF16), x), consts, lw)
    edge = _each(lambda x, rv: x[0:1] if rv else x[c - 1:c], g, rev)
    yield
    kg = _each(lambda x, gg, l: st(x * jnp.exp(gg - l)), kk, g, lw)
    rg = _each(lambda x, gg: st(x * jnp.exp(gg)), r, g)
    e_out = _each(lambda gg: jnp.exp(-gg), g)
    bi = _each(lambda x, e: st(x * e), beta, e_out)
    ki = _each(lambda x, e: st(x * e), k, e_out)
    vs = _each(st, v)
    a_all = _each(lambda a, b, cc, d: _mm(jnp.concatenate([a, b], 0), jnp.concatenate([cc, d], 0), NT, RWKV_PASSES_SCORE),
                  kg, rg, bi, ki)
    yield
    n = _each(lambda m, a: m * a[:2 * c, :2 * c], strict, a_all)
    a_ak = _each(lambda m, a: m * a[:2 * c, 2 * c:], strict, a_all)
    a_rb = _each(lambda m, a: m * a[2 * c:, :2 * c], incl, a_all)
    a_rk = _each(lambda m, a: m * a[2 * c:, 2 * c:], incl, a_all)
    av = _each(lambda a, b, x: _mm(jnp.concatenate([a, b], 0), x, passes=RWKV_PASSES_SCORE), a_ak, a_rk, vs)
    yield
    eye = (lax.broadcasted_iota(jnp.int32, (2 * c, 2 * c), 0) == lax.broadcasted_iota(jnp.int32, (2 * c, 2 * c), 1)).astype(F32)
    tinv = _each(lambda a: eye - a, n)
    power = _each(lambda a: _mm(a, a, passes=RWKV_PASSES_SOLVE), n)
    yield
    span = 4
    while span < c:
        both = _each(lambda p, t: _mm(p, jnp.concatenate([p, t], axis=1), passes=RWKV_PASSES_SOLVE), power, tinv)
        power = [b[:, :2 * c] for b in both]
        tinv = _each(lambda t, b: t + b[:, 2 * c:], tinv, both)
        span *= 2
        yield
    tinv = _each(lambda p, t: t + _mm(p, t, passes=RWKV_PASSES_SOLVE), power, tinv)
    yield
    x = _each(lambda t, a, b: _mm(t, jnp.concatenate([a, b[:2 * c]], axis=1), passes=RWKV_PASSES_SOLVE), tinv, kg, av)
    yield
    e_edge = _each(lambda ed, gg: jnp.exp(ed - gg), edge, g)
    kv = _each(lambda a, b, e: _mm(a, st(b * e), TN, RWKV_PASSES_STATE), vs, k, e_edge)
    yield
    bgc = _each(lambda b, e: st(b * e), beta, e_edge)
    return [((xx[:, :LANES], rr, xx[:, LANES:], ab, a[2 * c:], bg, kvv), jnp.exp(ed))
            for xx, rr, ab, a, bg, kvv, ed in zip(x, rg, a_rb, av, bgc, kv, edge)]


def _rwkv_chunks_state(states, parts, decays):
    w1, rg, x2, a_rb, y0, bgc, kv = (list(col) for col in zip(*parts))
    c2 = w1[0].shape[0]
    hs = _each(lambda a, b, s: _mm(jnp.concatenate([a, b], 0), s, NT, RWKV_PASSES_STATE), w1, rg, states)
    yield
    u = _each(lambda h, x: -(h[:c2] + x), hs, x2)
    s = _each(lambda s_, d, uu, b, kv_: s_ * d + _mm(uu, b, TN, RWKV_PASSES_STATE) + kv_, states, decays, u, bgc, kv)
    yield
    y = _each(lambda h, y_, a, uu: h[c2:] + y_ + _mm(a, uu, passes=RWKV_PASSES_STATE), hs, y0, a_rb, u)
    yield
    return [yy[:c2 // 2] + yy[c2 // 2:] for yy in y], s


def _params(*semantics):
    return pltpu.CompilerParams(dimension_semantics=semantics, vmem_limit_bytes=VMEM_LIMIT)


def _full(shape):
    return pl.BlockSpec(shape, lambda *_: (0,) * len(shape))


def _modulated_norm(x, norm_w, scale, shift):
    y = x * lax.rsqrt(jnp.mean(x * x, axis=-1, keepdims=True) + RMS_EPS)
    return (y * norm_w) * (1.0 + scale) + shift


def _mod_kernel(c_ref, w_ref, b_ref, o_ref):
    o_ref[...] = _mm(_silu(c_ref[...]), w_ref[...], passes=3) + b_ref[...]


def _modulation(cvec, ada_w, ada_b):
    n = ada_w.shape[1]
    tn = n // 4
    return pl.pallas_call(
        _mod_kernel,
        grid=(n // tn,),
        in_specs=[_full(cvec.shape), pl.BlockSpec((D_MODEL, tn), lambda j: (0, j)), pl.BlockSpec((1, tn), lambda j: (0, j))],
        out_specs=pl.BlockSpec((cvec.shape[0], tn), lambda j: (0, j)),
        out_shape=jax.ShapeDtypeStruct((cvec.shape[0], n), F32),
        compiler_params=_params("arbitrary"),
        name="modulation",
    )(cvec, ada_w, ada_b)


def _in_proj_kernel(x_ref, xp_ref, xn_ref, mod_ref, nw_ref, wa_ref, wb_ref, cw_ref, ua_ref, ub_ref, *, seq_tiles):
    i = pl.program_id(0)
    mod = mod_ref[...]
    shift, scale = mod[:, 0:D_MODEL], mod[:, D_MODEL:2 * D_MODEL]
    nw = nw_ref[...]
    h = _modulated_norm(x_ref[...], nw, scale, shift).astype(BF16)
    ua_ref[...] = _dot(h, wa_ref[...])
    halo = jnp.concatenate([xp_ref[...], xn_ref[...]], axis=0)
    hh = _modulated_norm(halo, nw, scale, shift).astype(BF16)
    ub_all = _dot(jnp.concatenate([h, hh], axis=0), wb_ref[...])
    n = h.shape[0]
    ub, ubh = ub_all[:n], ub_all[n:]
    first = (i % seq_tiles) == 0
    last = (i % seq_tiles) == seq_tiles - 1
    prev_row = jnp.where(first, 0.0, ubh[7:8])
    next_row = jnp.where(last, 0.0, ubh[8:9])
    rows = lax.broadcasted_iota(jnp.int32, ub.shape, 0)
    below = jnp.where(rows == 0, prev_row, pltpu.roll(ub, 1, 0))
    above = jnp.where(rows == n - 1, next_row, pltpu.roll(ub, n - 1, 0))
    cw = cw_ref[...]
    ub_ref[...] = cw[0:1] * below + cw[1:2] * ub + cw[2:3] * above


def _in_proj(x, mod, norm_w, w_a, w_b, conv_w, seq_tiles, mod_of_tile):
    t = x.shape[0]
    nt = t // ROW_TILE
    sub = ROW_TILE // 8
    return pl.pallas_call(
        functools.partial(_in_proj_kernel, seq_tiles=seq_tiles),
        grid=(nt,),
        in_specs=[
            pl.BlockSpec((ROW_TILE, D_MODEL), lambda i: (i, 0)),
            pl.BlockSpec((8, D_MODEL), lambda i: (jnp.maximum(i * sub - 1, 0), 0)),
            pl.BlockSpec((8, D_MODEL), lambda i: (jnp.minimum((i + 1) * sub, nt * sub - 1), 0)),
            pl.BlockSpec((None, 1, 6 * D_MODEL), lambda i: (mod_of_tile(i), 0, 0)),
            _full((1, D_MODEL)), _full(w_a.shape), _full(w_b.shape), _full(conv_w.shape),
        ],
        out_specs=[pl.BlockSpec((ROW_TILE, P_A), lambda i: (i, 0)), pl.BlockSpec((ROW_TILE, P_B_PAD), lambda i: (i, 0))],
        out_shape=[jax.ShapeDtypeStruct((t, P_A), F32), jax.ShapeDtypeStruct((t, P_B_PAD), F32)],
        compiler_params=_params("parallel"),
        name="in_proj",
    )(x, x, x, mod, norm_w, w_a, w_b, conv_w)


def _hgrn_kernel(q_ref, i_ref, zf_ref, zb_ref, g_ref, lb_ref, nw_ref, s0_ref, m_ref, mask_ref,
                 o_ref, sout_ref, acc_s, st_ref, *, ns, nc, group, block):
    c = CHUNK
    seq_len = nc * c
    chains = [(s, d) for s in range(ns) for d in range(2)]
    p0, p1 = lb_ref[0], lb_ref[1]
    mx = jnp.maximum(p0, p1)
    e0, e1 = jnp.exp(p0 - mx), jnp.exp(p1 - mx)
    lb = e0 / (e0 + e1)
    for ch, (s, d) in enumerate(chains):
        st_ref[ch] = s0_ref[s, d]

    def rows_of(gi, j, s, d):
        cf = gi * group + j
        ci = cf if d == 0 else nc - 1 - cf
        return pl.ds(pl.multiple_of(s * seq_len + ci * c, c), c)

    def scan_group(gi, carry):
        units = []
        for j in range(group):
            for s, d in chains:
                rows = rows_of(gi, j, s, d)
                lo = lb[d:d + 1]
                f = lo + (1.0 - lo) * _sigmoid((zf_ref, zb_ref)[d][rows, :])
                units.append((_silu(q_ref[rows, :]), 1.0 - f, i_ref[rows, :], jnp.log(f), m_ref[d], mask_ref[d], d == 1))
        parts, = _interleave(_hgrn_chunks_local(units))
        st = [st_ref[ch] for ch in range(len(chains))]
        for j in range(group):
            for ch, (s, d) in enumerate(chains):
                o, st[ch] = _hgrn_chunk_state(st[ch], parts[j * len(chains) + ch])
                acc_s[rows_of(gi, j, s, d), :] += o
        for ch, state in enumerate(st):
            st_ref[ch] = state
        return carry

    acc_s[...] = jnp.zeros(acc_s.shape, F32)
    lax.fori_loop(0, nc // group, scan_group, 0)

    def finish(bi, carry):
        rows = pl.ds(pl.multiple_of(bi * block, block), block)
        o = acc_s[rows, :]
        o = o * lax.rsqrt(jnp.mean(o * o, axis=-1, keepdims=True) + RMS_EPS) * nw_ref[...]
        o_ref[rows, :] = o * _silu(g_ref[rows, :])
        return carry

    lax.fori_loop(0, ns * seq_len // block, finish, 0, unroll=2 if ns * seq_len // block % 2 == 0 else 1)
    for ch, (s, d) in enumerate(chains):
        sout_ref[s, d] = st_ref[ch]


def _hgrn_mixer(u_a, lb_raw, norm_w, s0t, seq_len):
    t = u_a.shape[0]
    nb = t // seq_len
    nc = seq_len // CHUNK
    group = min(HGRN_UNITS // 2, nc)
    ns = max(1, min(HGRN_UNITS // (2 * group), nb))
    assert nc % group == 0 and nb % ns == 0
    rows = ns * seq_len
    mats, masks = _hgrn_consts(CHUNK)
    mats = jnp.asarray(mats, BF16)
    masks = jnp.asarray(masks, F32)
    col = lambda part: pl.BlockSpec((rows, HEAD_A), lambda b, h: (b, part * N_HEADS_A + h))
    state_spec = pl.BlockSpec((ns, 2, None, HEAD_A, HEAD_A), lambda b, h: (b, 0, h, 0, 0))
    return pl.pallas_call(
        functools.partial(_hgrn_kernel, ns=ns, nc=nc, group=group, block=min(MIXER_ROW_BLOCK, rows)),
        grid=(nb // ns, N_HEADS_A),
        in_specs=[col(0), col(1), col(2), col(3), col(4),
                  pl.BlockSpec((2, 2, HEAD_A), lambda b, h: (0, 0, h)),
                  pl.BlockSpec((1, HEAD_A), lambda b, h: (0, h)),
                  state_spec, _full(mats.shape), _full(masks.shape)],
        out_specs=[pl.BlockSpec((rows, HEAD_A), lambda b, h: (b, h)), state_spec],
        out_shape=[jax.ShapeDtypeStruct((t, WIDTH_A), F32), jax.ShapeDtypeStruct((nb, 2, N_HEADS_A, HEAD_A, HEAD_A), F32)],
        scratch_shapes=[pltpu.VMEM((rows, HEAD_A), F32), pltpu.VMEM((2 * ns, HEAD_A, HEAD_A), F32)],
        compiler_params=_params("parallel", "parallel"),
        name="hgrn2_scan",
    )(u_a, u_a, u_a, u_a, u_a, lb_raw, norm_w, s0t, mats, masks)


def _mm_exact_rhs(x, m):
    n = x.shape[0]
    both = _dot(jnp.concatenate(_split(x), axis=0), m)
    return both[:n] + both[n:]


def _rwkv_kernel(r_ref, k_ref, v_ref, lora_ref, w0_ref, a0_ref, kkw_ref, ka_ref, rk_ref, lnw_ref, lnb_ref,
                 wdec_ref, wa_ref, wg_ref, s0_ref, consts_ref, seg_ref,
                 o_ref, sout_ref, lwf_s, lwb_s, kk_s, beta_s, kmod_s, y_s, loca_s, deca_s, locb_s, decb_s, st_ref,
                 *, ns, nc, group, block):
    c = CHUNK
    seq_len = nc * c
    chains = [(s, d) for s in range(ns) for d in range(2)]
    seg = seg_ref[...]

    for ch, (s, d) in enumerate(chains):
        st_ref[ch] = jnp.zeros((LANES, LANES), F32)
        st_ref[ch, 0:HEAD_B, 0:HEAD_B] = s0_ref[s, d, 0]
        st_ref[ch, HEAD_B:LANES, HEAD_B:LANES] = s0_ref[s, d, 1]

    def block_of(bi):
        return pl.ds(pl.multiple_of(bi * block, block), block)

    def prepare(bi, carry):
        rows = block_of(bi)
        lora = lora_ref[rows, :]
        th, tl = _split(jnp.tanh(lora))
        w_hi, w_lo = wdec_ref[0], wdec_ref[1]
        dec = _dot(th, w_hi) + (_dot(th, w_lo) + _dot(tl, w_hi))
        w0 = w0_ref[...]
        lwf_s[rows, :] = -DECAY_SCALE * _sigmoid(w0[0:1] + dec[:, :LANES])
        lwb_s[rows, :] = -DECAY_SCALE * _sigmoid(w0[1:2] + dec[:, LANES:])
        a = _sigmoid(a0_ref[...] + _mm(lora, wa_ref[...]))
        k = k_ref[rows, :]
        kk = k * kkw_ref[...]
        kk = kk * lax.rsqrt(_mm(kk * kk, seg) + 1e-12)
        kk_s[rows, :] = kk
        beta_s[rows, :] = kk * a
        kmod_s[rows, :] = k * (1.0 + (a - 1.0) * ka_ref[...])
        return carry

    lax.fori_loop(0, ns * seq_len // block, prepare, 0, unroll=4)

    def rows_of(gi, j, s, d):
        cf = gi * group + j
        ci = cf if d == 0 else nc - 1 - cf
        return pl.ds(pl.multiple_of(s * seq_len + ci * c, c), c)

    def local_stage(gi, loc, dec):
        units = []
        for j in range(group):
            for s, d in chains:
                rows = rows_of(gi, j, s, d)
                units.append((r_ref[rows, :], kmod_s[rows, :], v_ref[rows, :], kk_s[rows, :], beta_s[rows, :],
                              (lwf_s, lwb_s)[d][rows, :], consts_ref[d], d == 1))
        results = yield from _rwkv_chunks_local(units)
        for u, (parts, decay) in enumerate(results):
            j, ch = divmod(u, len(chains))
            for idx, part in enumerate(parts):
                loc[ch, j, idx] = part
            dec[ch, j] = decay

    def state_stage(gi, loc, dec):
        states = [st_ref[ch] for ch in range(len(chains))]
        for j in range(group):
            parts = [[loc[ch, j, idx] for idx in range(RWKV_LOCAL_PARTS)] for ch in range(len(chains))]
            ys, states = yield from _rwkv_chunks_state(states, parts, [dec[ch, j] for ch in range(len(chains))])
            for y, (s, d) in zip(ys, chains):
                y_s[rows_of(gi, j, s, d), :] += y
        for ch, state in enumerate(states):
            st_ref[ch] = state

    y_s[...] = jnp.zeros(y_s.shape, F32)
    ng = nc // group
    _interleave(local_stage(0, loca_s, deca_s))

    def two_groups(p, carry):
        _interleave(local_stage(2 * p + 1, locb_s, decb_s), state_stage(2 * p, loca_s, deca_s))
        _interleave(local_stage(2 * p + 2, loca_s, deca_s), state_stage(2 * p + 1, locb_s, decb_s))
        return carry

    lax.fori_loop(0, ng // 2 - 1, two_groups, 0)
    _interleave(local_stage(ng - 1, locb_s, decb_s), state_stage(ng - 2, loca_s, deca_s))
    _interleave(state_stage(ng - 1, locb_s, decb_s))

    def finish(bi, carry):
        rows = block_of(bi)
        y = y_s[rows, :]
        inv_n = 1.0 / HEAD_B
        mu = _mm_exact_rhs(y, seg) * inv_n
        dy = y - mu
        var = _mm(dy * dy, seg) * inv_n
        yn = dy * lax.rsqrt(var + GN_EPS) * lnw_ref[...] + lnb_ref[...]
        v = v_ref[rows, :]
        bonus = _mm(r_ref[rows, :] * kmod_s[rows, :] * rk_ref[...], seg) * v
        g = _mm(_sigmoid(lora_ref[rows, :]), wg_ref[...])
        o_ref[rows, :] = (yn + bonus) * g
        return carry

    lax.fori_loop(0, ns * seq_len // block, finish, 0, unroll=4)
    for ch, (s, d) in enumerate(chains):
        sout_ref[s, d, 0] = st_ref[ch, 0:HEAD_B, 0:HEAD_B]
        sout_ref[s, d, 1] = st_ref[ch, HEAD_B:LANES, HEAD_B:LANES]


def _rwkv_mixer(u_b, p, s0, seq_len):
    t = u_b.shape[0]
    nb = t // seq_len
    nc = seq_len // CHUNK
    group = max(1, min(RWKV_UNITS // 2, nc // 4))
    ns = max(1, min(RWKV_UNITS // (2 * group), nb))
    assert nc % (2 * group) == 0 and nb % ns == 0
    rows = ns * seq_len
    block = min(MIXER_ROW_BLOCK, rows)
    pairs = N_HEADS_B // 2
    consts = jnp.asarray(_rwkv_consts(CHUNK), F32)
    lane = np.arange(LANES) // HEAD_B
    seg = jnp.asarray(lane[:, None] == lane[None, :], BF16)
    col = lambda part: pl.BlockSpec((rows, LANES), lambda b, h: (b, part * pairs + h))
    vec = lambda n: pl.BlockSpec((n, LANES), lambda b, h: (0, h))
    state_spec = pl.BlockSpec((ns, 2, 2, HEAD_B, HEAD_B), lambda b, h: (b, 0, h, 0, 0))
    scr = lambda: pltpu.VMEM((rows, LANES), F32)
    loc = lambda: pltpu.VMEM((2 * ns, group, RWKV_LOCAL_PARTS, LANES, LANES), F32)
    dec = lambda: pltpu.VMEM((2 * ns, group, 1, LANES), F32)
    return pl.pallas_call(
        functools.partial(_rwkv_kernel, ns=ns, nc=nc, group=group, block=block),
        grid=(nb // ns, pairs),
        in_specs=[col(0), col(1), col(2),
                  pl.BlockSpec((rows, LORA_COLS), lambda b, h: (b, 3 * WIDTH_B // LORA_COLS)),
                  vec(2), vec(1), vec(1), vec(1), vec(1), vec(1), vec(1),
                  pl.BlockSpec((None, 2, LORA_COLS, 2 * LANES), lambda b, h: (h, 0, 0, 0)),
                  pl.BlockSpec((LORA_COLS, LANES), lambda b, h: (0, h)),
                  pl.BlockSpec((LORA_COLS, LANES), lambda b, h: (0, h)),
                  state_spec, _full(consts.shape), _full(seg.shape)],
        out_specs=[pl.BlockSpec((rows, LANES), lambda b, h: (b, h)), state_spec],
        out_shape=[jax.ShapeDtypeStruct((t, WIDTH_B), F32), jax.ShapeDtypeStruct((nb, 2, N_HEADS_B, HEAD_B, HEAD_B), F32)],
        scratch_shapes=[scr(), scr(), scr(), scr(), scr(), scr(), loc(), dec(), loc(), dec(),
                        pltpu.VMEM((2 * ns, LANES, LANES), F32)],
        compiler_params=_params("parallel", "parallel"),
        name="rwkv7_scan",
    )(u_b, u_b, u_b, u_b, p["w0"], p["a0"], p["k_k"], p["k_a"], p["r_k"], p["ln_w"], p["ln_b"],
      p["w_dec"], p["w_a"], p["w_g"], s0, consts, seg)


def _mix_out_kernel(oa_ref, ob_ref, x_ref, mod_ref, woa_ref, wob_ref, nw_ref, wg_ref, wu_ref, cw_ref,
                    x1_ref, c_ref, u_ref, *, grid_conv):
    mod = mod_ref[...]
    gate1 = mod[:, 2 * D_MODEL:3 * D_MODEL]
    shift2, scale2 = mod[:, 3 * D_MODEL:4 * D_MODEL], mod[:, 4 * D_MODEL:5 * D_MODEL]
    mix = _dot(oa_ref[...].astype(BF16), woa_ref[...]) + _dot(ob_ref[...].astype(BF16), wob_ref[...])
    x1 = x_ref[...] + gate1 * mix
    x1_ref[...] = x1
    h = _modulated_norm(x1, nw_ref[...], scale2, shift2).astype(BF16)
    n = h.shape[0]
    width = GRID_W if grid_conv else n
    col = lax.broadcasted_iota(jnp.int32, (width, FFN_COL_BLOCK), 0)
    blocks = [slice(j * FFN_COL_BLOCK, (j + 1) * FFN_COL_BLOCK) for j in range(D_FF // FFN_COL_BLOCK)]
    g_next = _dot(h, wg_ref[:, blocks[0]])
    for j, cols in enumerate(blocks):
        g_all = g_next
        u_ref[:, cols] = _dot(h, wu_ref[:, cols]).astype(BF16)
        if j + 1 < len(blocks):
            g_next = _dot(h, wg_ref[:, blocks[j + 1]])
        cw = cw_ref[:, cols]
        for r0 in range(0, n, width):
            g = g_all[r0:r0 + width]
            left = jnp.where(col == 0, 0.0, pltpu.roll(g, 1, 0))
            right = jnp.where(col == width - 1, 0.0, pltpu.roll(g, width - 1, 0))
            for slot, dr in enumerate((0, 1, 2) if grid_conv else (1,)):
                c_ref[slot, r0:r0 + width, cols] = (cw[3 * dr:3 * dr + 1] * left + cw[3 * dr + 1:3 * dr + 2] * g
                                                    + cw[3 * dr + 2:3 * dr + 3] * right).astype(BF16)


def _mix_out(o_a, o_b, x, mod, w_out_a, w_out_b, norm_w, w_gate, w_up, conv_w, grid_conv, mod_of_tile):
    t = x.shape[0]
    nr = 3 if grid_conv else 1
    row = lambda n: pl.BlockSpec((ROW_TILE, n), lambda i: (i, 0))
    return pl.pallas_call(
        functools.partial(_mix_out_kernel, grid_conv=grid_conv),
        grid=(t // ROW_TILE,),
        in_specs=[row(WIDTH_A), row(WIDTH_B), row(D_MODEL),
                  pl.BlockSpec((None, 1, 6 * D_MODEL), lambda i: (mod_of_tile(i), 0, 0)),
                  _full(w_out_a.shape), _full(w_out_b.shape), _full((1, D_MODEL)), _full(w_gate.shape), _full(w_up.shape),
                  _full(conv_w.shape)],
        out_specs=[row(D_MODEL), pl.BlockSpec((nr, ROW_TILE, D_FF), lambda i: (0, i, 0)), row(D_FF)],
        out_shape=[jax.ShapeDtypeStruct((t, D_MODEL), F32), jax.ShapeDtypeStruct((nr, t, D_FF), BF16),
                   jax.ShapeDtypeStruct((t, D_FF), BF16)],
        compiler_params=_params("parallel"),
        name="mix_out_ffn_in",
    )(o_a, o_b, x, mod, w_out_a, w_out_b, norm_w, w_gate, w_up, conv_w)


def _gelu_tanh(x):
    half = 0.5 * x
    return half + half * jnp.tanh(x * (0.7978845608028654 + (0.7978845608028654 * 0.044715) * (x * x)))


def _ffn_out_kernel(*refs, seq_tiles, grid_conv):
    if grid_conv:
        c_ref, ca_ref, cb_ref, u_ref, x1_ref, mod_ref, bias_ref, wd_ref, fw_ref, y_ref = refs
    else:
        c_ref, u_ref, x1_ref, mod_ref, bias_ref, wd_ref, fw_ref, y_ref = refs
    f32 = lambda v: v.astype(F32)
    if grid_conv:
        i = pl.program_id(0)
        top = (i % seq_tiles) == 0
        bottom = (i % seq_tiles) == seq_tiles - 1
        n = c_ref.shape[1]
        up_rows = jnp.concatenate([jnp.where(top, 0.0, f32(ca_ref[...])), f32(c_ref[0, :n - GRID_W, :])], axis=0)
        down_rows = jnp.concatenate([f32(c_ref[2, GRID_W:, :]), jnp.where(bottom, 0.0, f32(cb_ref[...]))], axis=0)
        gt = f32(c_ref[1]) + up_rows + down_rows + bias_ref[...]
    else:
        gt = f32(c_ref[0]) + bias_ref[...]
    act = (_gelu_tanh(gt) * f32(u_ref[...])).astype(BF16)
    mod = mod_ref[...]
    gate2 = mod[:, 5 * D_MODEL:6 * D_MODEL]
    x2 = x1_ref[...] + gate2 * _dot(act, wd_ref[...])
    y_ref[...] = x2 * lax.rsqrt(jnp.mean(x2 * x2, axis=-1, keepdims=True) + RMS_EPS) * fw_ref[...]


def _ffn_out(c, up, x1, mod, conv_b, w_down, final_w, seq_tiles, grid_conv, mod_of_tile):
    t = x1.shape[0]
    nt = t // ROW_TILE
    per = ROW_TILE // GRID_W
    row = lambda n: pl.BlockSpec((ROW_TILE, n), lambda i: (i, 0))
    c_specs = [pl.BlockSpec((c.shape[0], ROW_TILE, D_FF), lambda i: (0, i, 0))]
    c_args = [c]
    if grid_conv:
        c_specs += [pl.BlockSpec((None, GRID_W, D_FF), lambda i: (0, jnp.maximum(i * per - 1, 0), 0)),
                    pl.BlockSpec((None, GRID_W, D_FF), lambda i: (2, jnp.minimum((i + 1) * per, nt * per - 1), 0))]
        c_args += [c, c]
    return pl.pallas_call(
        functools.partial(_ffn_out_kernel, seq_tiles=seq_tiles, grid_conv=grid_conv),
        grid=(nt,),
        in_specs=c_specs + [row(D_FF), row(D_MODEL),
                            pl.BlockSpec((None, 1, 6 * D_MODEL), lambda i: (mod_of_tile(i), 0, 0)),
                            _full((1, D_FF)), _full(w_down.shape), _full((1, D_MODEL))],
        out_specs=row(D_MODEL),
        out_shape=jax.ShapeDtypeStruct((t, D_MODEL), F32),
        compiler_params=_params("parallel"),
        name="ffn_out",
    )(*c_args, up, x1, mod, conv_b, w_down, final_w)


def _block(x, seq_len, mod, mod_of_tile, s_hgrn_t, s_rwkv, grid_conv, w):
    seq_tiles = seq_len // ROW_TILE
    u_a, u_b = _in_proj(x, mod, w["norm_mix_w"], w["w_in_a"], w["w_in_b"], w["rwkv_conv"], seq_tiles, mod_of_tile)
    o_a, s_h = _hgrn_mixer(u_a, w["hgrn_lb"], w["hgrn_norm_w"], s_hgrn_t, seq_len)
    o_b, s_r = _rwkv_mixer(u_b, w["rwkv"], s_rwkv, seq_len)
    x1, conv_sums, up = _mix_out(o_a, o_b, x, mod, w["w_out_a"], w["w_out_b"], w["norm_ffn_w"], w["ffn_w_gate"],
                                 w["ffn_w_up"], w["ffn_conv"], grid_conv, mod_of_tile)
    y = _ffn_out(conv_sums, up, x1, mod, w["ffn_conv_b"], w["ffn_w_down"], w["final_norm_w"],
                 seq_tiles, grid_conv, mod_of_tile)
    return y, s_h, s_r


def _place_rows(w, start, total):
    return jnp.zeros((total, w.shape[1]), w.dtype).at[start:start + w.shape[0]].set(w)


def kernel(x_prompt, x_sample, state_hgrn, state_rwkv, c, c_ctx, ada_w, ada_b, norm_mix_w, w_in, hgrn_lb, hgrn_norm_w, rwkv_conv, rwkv_w0, rwkv_w2, rwkv_a0, rwkv_a2, rwkv_g2, rwkv_k_k, rwkv_k_a, rwkv_r_k, rwkv_ln_w, rwkv_ln_b, w_out, norm_ffn_w, ffn_w_gate, ffn_w_up, ffn_conv, ffn_conv_b, ffn_w_down, final_norm_w):
    assert w_in.shape[0] == 1, "one trunk layer"
    b_ctx, t_ctx, _ = x_prompt.shape
    b_lat, t_lat, _ = x_sample.shape
    row = lambda v: v.reshape(1, -1)
    pad_cols = P_B_PAD - P_B
    w_dec = jnp.concatenate([_place_rows(rwkv_w2[0, 0], 0, LORA_COLS), _place_rows(rwkv_w2[0, 1], LORA_W, LORA_COLS)], axis=1)
    pairs = N_HEADS_B // 2
    w_dec = w_dec.reshape(LORA_COLS, 2, pairs, LANES).transpose(2, 0, 1, 3).reshape(pairs, LORA_COLS, 2 * LANES)
    w_dec = jnp.stack(_split(w_dec), axis=1)
    weights = dict(
        norm_mix_w=row(norm_mix_w[0]),
        w_in_a=w_in[0, :, :P_A].astype(BF16),
        w_in_b=jnp.pad(w_in[0, :, P_A:], ((0, 0), (0, pad_cols))).astype(BF16),
        rwkv_conv=jnp.pad(rwkv_conv[0], ((0, 0), (0, pad_cols))),
        hgrn_lb=hgrn_lb,
        hgrn_norm_w=row(hgrn_norm_w[0]),
        rwkv=dict(w0=rwkv_w0[0], a0=row(rwkv_a0[0]), k_k=row(rwkv_k_k[0]), k_a=row(rwkv_k_a[0]), r_k=row(rwkv_r_k[0]),
                  ln_w=row(rwkv_ln_w[0]), ln_b=row(rwkv_ln_b[0]), w_dec=w_dec,
                  w_a=_place_rows(rwkv_a2[0], 2 * LORA_W, LORA_COLS),
                  w_g=_place_rows(rwkv_g2[0], 2 * LORA_W + LORA_A, LORA_COLS)),
        w_out_a=w_out[0, :WIDTH_A].astype(BF16),
        w_out_b=w_out[0, WIDTH_A:].astype(BF16),
        norm_ffn_w=row(norm_ffn_w[0]),
        ffn_w_gate=ffn_w_gate[0].astype(BF16),
        ffn_w_up=ffn_w_up[0].astype(BF16),
        ffn_conv=ffn_conv[0].reshape(9, D_FF),
        ffn_conv_b=row(ffn_conv_b[0]),
        ffn_w_down=ffn_w_down[0].astype(BF16),
        final_norm_w=row(final_norm_w),
    )
    cvec = jnp.concatenate([c_ctx[None, :], c, jnp.zeros((8 - 1 - b_lat, D_MODEL), F32)], axis=0)
    mod = _modulation(cvec, ada_w[0], row(ada_b[0])).reshape(8, 1, 6 * D_MODEL)

    zeros_h = jnp.zeros((b_ctx, 2, N_HEADS_A, HEAD_A, HEAD_A), F32)
    zeros_r = jnp.zeros((b_ctx, 2, N_HEADS_B, HEAD_B, HEAD_B), F32)
    yp, s_h, s_r = _block(x_prompt.reshape(b_ctx * t_ctx, D_MODEL), t_ctx, mod, lambda i: 0, zeros_h, zeros_r, False, weights)
    lat_tiles = t_lat // ROW_TILE
    ys, _, _ = _block(x_sample.reshape(b_lat * t_lat, D_MODEL), t_lat, mod, lambda i: 1 + i // lat_tiles,
                      jnp.swapaxes(state_hgrn[:, 0], -1, -2), state_rwkv[:, 0], True, weights)
    y_prompt = yp.reshape(b_ctx, t_ctx, D_MODEL)
    y_sample = ys.reshape(b_lat, t_lat, D_MODEL)
    new_state_hgrn = jnp.swapaxes(s_h, -1, -2)[:, None]
    new_state_rwkv = s_r[:, None]
    return (y_prompt, y_sample, new_state_hgrn, new_state_rwkv)
```

```python
import functools

import numpy as np
import jax
import jax.numpy as jnp
from jax import lax
from jax.experimental import pallas as pl
from jax.experimental.pallas import tpu as pltpu

F32 = jnp.float32
BF16 = jnp.bfloat16

D_MODEL = 1024
GRID_W = 64
WIDTH_A = 512
HEAD_A = 128
N_HEADS_A = 4
WIDTH_B = 512
HEAD_B = 64
N_HEADS_B = 8
LORA_W = 32
LORA_A = 32
LORA_G = 96
D_FF = 2816
RMS_EPS = 1e-6
GN_EPS = 64e-5
DECAY_SCALE = 0.6065306597
P_A = 5 * WIDTH_A
P_B = 3 * WIDTH_B + 2 * LORA_W + LORA_A + LORA_G
LORA_COLS = 256
P_B_PAD = 3 * WIDTH_B + LORA_COLS

CHUNK = 64
LANES = 128
SUBLANES = 8
ROW_TILE = 256
FFN_COL_BLOCK = 256
VMEM_LIMIT = 56 * 1024 * 1024

NN = (((1,), (0,)), ((), ()))
NT = (((1,), (1,)), ((), ()))
TN = (((0,), (0,)), ((), ()))


def _dot(a, b, dims=NN):
    return lax.dot_general(a, b, dims, preferred_element_type=F32)


def _split(a):
    hi = a.astype(BF16)
    lo = (a - hi.astype(F32)).astype(BF16)
    return hi, lo


def _mm(a, b, dims=NN, passes=1):
    if passes == 1:
        return _dot(a.astype(BF16), b.astype(BF16), dims)
    ah, al = _split(a)
    bh, bl = _split(b)
    return _dot(ah, bh, dims) + (_dot(ah, bl, dims) + _dot(al, bh, dims))


def _mm_exact_lhs(m, x):
    n = x.shape[1]
    both = _dot(m, jnp.concatenate(_split(x), axis=1))
    return both[:, :n] + both[:, n:]


def _sigmoid(x):
    return 0.5 * jnp.tanh(0.5 * x) + 0.5


def _silu(x):
    return x * _sigmoid(x)


def _hgrn_level_consts(c, rev):
    t = np.arange(c)
    mats = [(t[None, :] <= t[:, None]).astype(np.float32)]
    masks = [np.eye(c, dtype=np.float32)]
    n = 1
    while n < c:
        blk, half = t // (2 * n), (t // n) % 2
        mid = blk * 2 * n + n
        m = np.zeros((c, c), np.float32)
        for row in range(c):
            if half[row] == 1:
                m[row, mid[row]:row + 1] = 1.0
            else:
                m[row, row + 1:mid[row]] = 1.0
        mats.append(m)
        masks.append(((blk[:, None] == blk[None, :]) & (half[:, None] == 1) & (half[None, :] == 0))
                     .astype(np.float32))
        n *= 2
    if rev:
        mats = [m[::-1, ::-1] for m in mats]
        masks = [m[::-1, ::-1] for m in masks]
    return np.concatenate(mats, 0), np.concatenate(masks, 0)


def _hgrn_consts(c):
    mf, kf = _hgrn_level_consts(c, False)
    mb, kb = _hgrn_level_consts(c, True)
    mat_levels = 1 + int(np.log2(SUBLANES))
    return np.stack([mf, mb])[:, :mat_levels * c], np.stack([kf, kb])


def _rwkv_consts(c):
    t = np.arange(c)
    out = []
    for rev in (False, True):
        incl = (t[None, :] <= t[:, None]) if not rev else (t[None, :] >= t[:, None])
        strict = (t[None, :] < t[:, None]) if not rev else (t[None, :] > t[:, None])
        z = np.zeros((c, c), bool)
        cum = np.block([[incl, z], [z, z]])
        out.append(np.stack([cum, np.block([[strict, z], [z, strict]]), np.block([[incl, z], [z, incl]])]))
    return np.stack(out).astype(np.float32)


def _each(fn, *cols):
    return [fn(*args) for args in zip(*cols)]


def _interleave(*gens):
    results = [None] * len(gens)
    live = dict(enumerate(gens))
    while live:
        for i in list(live):
            try:
                next(live[i])
            except StopIteration as stop:
                results[i] = stop.value
                del live[i]
    return results


def _pair_exponent(b, n, rev):
    parts = []
    for base in range(0, b.shape[0], 2 * n):
        first, second = b[base:base + n], b[base + n:base + 2 * n]
        if rev:
            r = jnp.broadcast_to(b[base + n:base + n + 1], first.shape)
            parts += [first - r, r - second]
        else:
            r = jnp.broadcast_to(b[base + n - 1:base + n], first.shape)
            parts += [r - first, second - r]
    return jnp.concatenate(parts, axis=0)


def _hgrn_chunks_local(units):
    q, k, v, g, m, masks, rev = (list(col) for col in zip(*units))
    c = q[0].shape[0]
    levels = masks[0].shape[0] // c
    mat_levels = m[0].shape[0] // c
    e_all = _each(_mm_exact_lhs, m, g)
    yield
    b = [x[:c] for x in e_all]
    edge = _each(lambda x, rv: x[0:1] if rv else x[c - 1:c], b, rev)
    kv = _each(lambda vv, kk, ed, bb: _mm(vv, kk * jnp.exp(ed - bb), TN), v, k, edge, b)
    yield
    sc = _each(lambda mk, a, kk: mk[:c] * _mm(a, kk, NT), masks, q, k)
    yield
    for l in range(1, levels):
        if l < mat_levels:
            e = _each(lambda x: jnp.exp(x[l * c:(l + 1) * c]), e_all)
        else:
            e = _each(lambda bb, rv: jnp.exp(_pair_exponent(bb, 1 << (l - 1), rv)), b, rev)
        sc = _each(lambda s, mk, a, kk, ee: s + mk[l * c:(l + 1) * c] * _mm(a * ee, kk * ee, NT), sc, masks, q, k, e)
        yield
    intra = _each(_mm, sc, v)
    yield
    return [(ii, qq * jnp.exp(bb), kvv, jnp.exp(ed)) for ii, qq, bb, kvv, ed in zip(intra, q, b, kv, edge)]


def _hgrn_chunk_state(st, part):
    intra, qb, kv, decay = part
    return intra + _mm(qb, st, NT), st * decay + kv


def _stack_heads(x, lane_head):
    return jnp.concatenate([jnp.where(lane_head == 0, x, 0.0), jnp.where(lane_head == 1, x, 0.0)], axis=0)


RWKV_PASSES_SCORE = 1
RWKV_PASSES_SOLVE = 1
RWKV_PASSES_STATE = 1
RWKV_LOCAL_PARTS = 7
RWKV_UNITS = 8
HGRN_UNITS = 16
MIXER_ROW_BLOCK = 256


def _rwkv_chunks_local(units):
    c = units[0][0].shape[0]
    lane_head = lax.broadcasted_iota(jnp.int32, (c, LANES), 1) // HEAD_B
    st = lambda x: _stack_heads(x, lane_head)
    r, k, v, kk, beta, lw, consts, rev = (list(col) for col in zip(*units))
    strict = [cs[1] for cs in consts]
    incl = [cs[2] for cs in consts]
    g = _each(lambda cs, x: _mm_exact_lhs(cs[0][:c, :c].astype(BF16), x), consts, lw)
    edge = _each(lambda x, rv: x[0:1] if rv else x[c - 1:c], g, rev)
    yield
    kg = _each(lambda x, gg, l: st(x * jnp.exp(gg - l)), kk, g, lw)
    rg = _each(lambda x, gg: st(x * jnp.exp(gg)), r, g)
    e_out = _each(lambda gg: jnp.exp(-gg), g)
    bi = _each(lambda x, e: st(x * e), beta, e_out)
    ki = _each(lambda x, e: st(x * e), k, e_out)
    vs = _each(st, v)
    a_all = _each(lambda a, b, cc, d: _mm(jnp.concatenate([a, b], 0), jnp.concatenate([cc, d], 0), NT, RWKV_PASSES_SCORE),
                  kg, rg, bi, ki)
    yield
    n = _each(lambda m, a: m * a[:2 * c, :2 * c], strict, a_all)
    a_ak = _each(lambda m, a: m * a[:2 * c, 2 * c:], strict, a_all)
    a_rb = _each(lambda m, a: m * a[2 * c:, :2 * c], incl, a_all)
    a_rk = _each(lambda m, a: m * a[2 * c:, 2 * c:], incl, a_all)
    av = _each(lambda a, b, x: _mm(jnp.concatenate([a, b], 0), x, passes=RWKV_PASSES_SCORE), a_ak, a_rk, vs)
    yield
    eye = (lax.broadcasted_iota(jnp.int32, (2 * c, 2 * c), 0) == lax.broadcasted_iota(jnp.int32, (2 * c, 2 * c), 1)).astype(F32)
    tinv = _each(lambda a: eye - a, n)
    power = _each(lambda a: _mm(a, a, passes=RWKV_PASSES_SOLVE), n)
    yield
    span = 4
    while span < c:
        both = _each(lambda p, t: _mm(p, jnp.concatenate([p, t], axis=1), passes=RWKV_PASSES_SOLVE), power, tinv)
        power = [b[:, :2 * c] for b in both]
        tinv = _each(lambda t, b: t + b[:, 2 * c:], tinv, both)
        span *= 2
        yield
    tinv = _each(lambda p, t: t + _mm(p, t, passes=RWKV_PASSES_SOLVE), power, tinv)
    yield
    x = _each(lambda t, a, b: _mm(t, jnp.concatenate([a, b[:2 * c]], axis=1), passes=RWKV_PASSES_SOLVE), tinv, kg, av)
    yield
    e_edge = _each(lambda ed, gg: jnp.exp(ed - gg), edge, g)
    kv = _each(lambda a, b, e: _mm(a, st(b * e), TN, RWKV_PASSES_STATE), vs, k, e_edge)
    yield
    bgc = _each(lambda b, e: st(b * e), beta, e_edge)
    return [((xx[:, :LANES], rr, xx[:, LANES:], ab, a[2 * c:], bg, kvv), jnp.exp(ed))
            for xx, rr, ab, a, bg, kvv, ed in zip(x, rg, a_rb, av, bgc, kv, edge)]


def _rwkv_chunks_state(states, parts, decays):
    w1, rg, x2, a_rb, y0, bgc, kv = (list(col) for col in zip(*parts))
    c2 = w1[0].shape[0]
    hs = _each(lambda a, b, s: _mm(jnp.concatenate([a, b], 0), s, NT, RWKV_PASSES_STATE), w1, rg, states)
    yield
    u = _each(lambda h, x: -(h[:c2] + x), hs, x2)
    s = _each(lambda s_, d, uu, b, kv_: s_ * d + _mm(uu, b, TN, RWKV_PASSES_STATE) + kv_, states, decays, u, bgc, kv)
    yield
    y = _each(lambda h, y_, a, uu: h[c2:] + y_ + _mm(a, uu, passes=RWKV_PASSES_STATE), hs, y0, a_rb, u)
    yield
    return [yy[:c2 // 2] + yy[c2 // 2:] for yy in y], s


def _params(*semantics):
    return pltpu.CompilerParams(dimension_semantics=semantics, vmem_limit_bytes=VMEM_LIMIT)


def _full(shape):
    return pl.BlockSpec(shape, lambda *_: (0,) * len(shape))


def _modulated_norm(x, norm_w, scale, shift):
    y = x * lax.rsqrt(jnp.mean(x * x, axis=-1, keepdims=True) + RMS_EPS)
    return (y * norm_w) * (1.0 + scale) + shift


def _mod_kernel(c_ref, w_ref, b_ref, o_ref):
    o_ref[...] = _mm(_silu(c_ref[...]), w_ref[...], passes=3) + b_ref[...]


def _modulation(cvec, ada_w, ada_b):
    n = ada_w.shape[1]
    tn = n // 4
    return pl.pallas_call(
        _mod_kernel,
        grid=(n // tn,),
        in_specs=[_full(cvec.shape), pl.BlockSpec((D_MODEL, tn), lambda j: (0, j)), pl.BlockSpec((1, tn), lambda j: (0, j))],
        out_specs=pl.BlockSpec((cvec.shape[0], tn), lambda j: (0, j)),
        out_shape=jax.ShapeDtypeStruct((cvec.shape[0], n), F32),
        compiler_params=_params("arbitrary"),
        name="modulation",
    )(cvec, ada_w, ada_b)


def _in_proj_kernel(x_ref, xp_ref, xn_ref, mod_ref, nw_ref, wa_ref, wb_ref, cw_ref, ua_ref, ub_ref, *, seq_tiles):
    i = pl.program_id(0)
    mod = mod_ref[...]
    shift, scale = mod[:, 0:D_MODEL], mod[:, D_MODEL:2 * D_MODEL]
    nw = nw_ref[...]
    h = _modulated_norm(x_ref[...], nw, scale, shift).astype(BF16)
    ua_ref[...] = _dot(h, wa_ref[...])
    halo = jnp.concatenate([xp_ref[...], xn_ref[...]], axis=0)
    hh = _modulated_norm(halo, nw, scale, shift).astype(BF16)
    ub_all = _dot(jnp.concatenate([h, hh], axis=0), wb_ref[...])
    n = h.shape[0]
    ub, ubh = ub_all[:n], ub_all[n:]
    first = (i % seq_tiles) == 0
    last = (i % seq_tiles) == seq_tiles - 1
    prev_row = jnp.where(first, 0.0, ubh[7:8])
    next_row = jnp.where(last, 0.0, ubh[8:9])
    rows = lax.broadcasted_iota(jnp.int32, ub.shape, 0)
    below = jnp.where(rows == 0, prev_row, pltpu.roll(ub, 1, 0))
    above = jnp.where(rows == n - 1, next_row, pltpu.roll(ub, n - 1, 0))
    cw = cw_ref[...]
    ub_ref[...] = cw[0:1] * below + cw[1:2] * ub + cw[2:3] * above


def _in_proj(x, mod, norm_w, w_a, w_b, conv_w, seq_tiles, mod_of_tile):
    t = x.shape[0]
    nt = t // ROW_TILE
    sub = ROW_TILE // 8
    return pl.pallas_call(
        functools.partial(_in_proj_kernel, seq_tiles=seq_tiles),
        grid=(nt,),
        in_specs=[
            pl.BlockSpec((ROW_TILE, D_MODEL), lambda i: (i, 0)),
            pl.BlockSpec((8, D_MODEL), lambda i: (jnp.maximum(i * sub - 1, 0), 0)),
            pl.BlockSpec((8, D_MODEL), lambda i: (jnp.minimum((i + 1) * sub, nt * sub - 1), 0)),
            pl.BlockSpec((None, 1, 6 * D_MODEL), lambda i: (mod_of_tile(i), 0, 0)),
            _full((1, D_MODEL)), _full(w_a.shape), _full(w_b.shape), _full(conv_w.shape),
        ],
        out_specs=[pl.BlockSpec((ROW_TILE, P_A), lambda i: (i, 0)), pl.BlockSpec((ROW_TILE, P_B_PAD), lambda i: (i, 0))],
        out_shape=[jax.ShapeDtypeStruct((t, P_A), F32), jax.ShapeDtypeStruct((t, P_B_PAD), F32)],
        compiler_params=_params("parallel"),
        name="in_proj",
    )(x, x, x, mod, norm_w, w_a, w_b, conv_w)


def _hgrn_kernel(*refs, ns, nc, group, block, zero_state):
    q_ref, i_ref, zf_ref, zb_ref, g_ref, lb_ref, nw_ref = refs[:7]
    s0_ref = None if zero_state else refs[7]
    m_ref, mask_ref, o_ref, sout_ref, acc_s, st_ref = refs[-6:]
    c = CHUNK
    seq_len = nc * c
    chains = [(s, d) for s in range(ns) for d in range(2)]
    p0, p1 = lb_ref[0], lb_ref[1]
    mx = jnp.maximum(p0, p1)
    e0, e1 = jnp.exp(p0 - mx), jnp.exp(p1 - mx)
    lb = e0 / (e0 + e1)
    for ch, (s, d) in enumerate(chains):
        st_ref[ch] = jnp.zeros((HEAD_A, HEAD_A), F32) if zero_state else s0_ref[s, d].T

    def rows_of(gi, j, s, d):
        cf = gi * group + j
        ci = cf if d == 0 else nc - 1 - cf
        return pl.ds(pl.multiple_of(s * seq_len + ci * c, c), c)

    def scan_group(gi, carry):
        units = []
        for j in range(group):
            for s, d in chains:
                rows = rows_of(gi, j, s, d)
                lo = lb[d:d + 1]
                f = lo + (1.0 - lo) * _sigmoid((zf_ref, zb_ref)[d][rows, :])
                units.append((_silu(q_ref[rows, :]), 1.0 - f, i_ref[rows, :], jnp.log(f), m_ref[d], mask_ref[d], d == 1))
        parts, = _interleave(_hgrn_chunks_local(units))
        st = [st_ref[ch] for ch in range(len(chains))]
        for j in range(group):
            for ch, (s, d) in enumerate(chains):
                o, st[ch] = _hgrn_chunk_state(st[ch], parts[j * len(chains) + ch])
                acc_s[rows_of(gi, j, s, d), :] += o
        for ch, state in enumerate(st):
            st_ref[ch] = state
        return carry

    acc_s[...] = jnp.zeros(acc_s.shape, F32)
    lax.fori_loop(0, nc // group, scan_group, 0)

    def finish(bi, carry):
        rows = pl.ds(pl.multiple_of(bi * block, block), block)
        o = acc_s[rows, :]
        o = o * lax.rsqrt(jnp.mean(o * o, axis=-1, keepdims=True) + RMS_EPS) * nw_ref[...]
        o_ref[rows, :] = o * _silu(g_ref[rows, :])
        return carry

    lax.fori_loop(0, ns * seq_len // block, finish, 0, unroll=2 if ns * seq_len // block % 2 == 0 else 1)
    for ch, (s, d) in enumerate(chains):
        sout_ref[s, d] = st_ref[ch].T


def _hgrn_mixer(u_a, lb_raw, norm_w, s0, seq_len):
    t = u_a.shape[0]
    nb = t // seq_len
    nc = seq_len // CHUNK
    group = min(HGRN_UNITS // 2, nc)
    ns = max(1, min(HGRN_UNITS // (2 * group), nb))
    assert nc % group == 0 and nb % ns == 0
    rows = ns * seq_len
    mats, masks = _hgrn_consts(CHUNK)
    mats = jnp.asarray(mats, BF16)
    masks = jnp.asarray(masks, F32)
    col = lambda part: pl.BlockSpec((rows, HEAD_A), lambda b, h: (b, part * N_HEADS_A + h))
    state_spec = pl.BlockSpec((ns, 2, None, HEAD_A, HEAD_A), lambda b, h: (b, 0, h, 0, 0))
    state_in = ([], []) if s0 is None else ([state_spec], [s0])
    return pl.pallas_call(
        functools.partial(_hgrn_kernel, ns=ns, nc=nc, group=group, block=min(MIXER_ROW_BLOCK, rows),
                          zero_state=s0 is None),
        grid=(nb // ns, N_HEADS_A),
        in_specs=[col(0), col(1), col(2), col(3), col(4),
                  pl.BlockSpec((2, 2, HEAD_A), lambda b, h: (0, 0, h)),
                  pl.BlockSpec((1, HEAD_A), lambda b, h: (0, h)),
                  *state_in[0], _full(mats.shape), _full(masks.shape)],
        out_specs=[pl.BlockSpec((rows, HEAD_A), lambda b, h: (b, h)), state_spec],
        out_shape=[jax.ShapeDtypeStruct((t, WIDTH_A), F32), jax.ShapeDtypeStruct((nb, 2, N_HEADS_A, HEAD_A, HEAD_A), F32)],
        scratch_shapes=[pltpu.VMEM((rows, HEAD_A), F32), pltpu.VMEM((2 * ns, HEAD_A, HEAD_A), F32)],
        compiler_params=_params("parallel", "parallel"),
        name="hgrn2_scan",
    )(u_a, u_a, u_a, u_a, u_a, lb_raw, norm_w, *state_in[1], mats, masks)


def _mm_exact_rhs(x, m):
    n = x.shape[0]
    both = _dot(jnp.concatenate(_split(x), axis=0), m)
    return both[:n] + both[n:]


def _rwkv_kernel(*refs, ns, nc, group, block, zero_state):
    (r_ref, k_ref, v_ref, lora_ref, w0_ref, a0_ref, kkw_ref, ka_ref, rk_ref, lnw_ref, lnb_ref,
     wdec_ref, wa_ref, wg_ref) = refs[:14]
    s0_ref = None if zero_state else refs[14]
    (consts_ref, seg_ref, o_ref, sout_ref, lwf_s, lwb_s, kk_s, beta_s, kmod_s, y_s,
     loca_s, deca_s, locb_s, decb_s, st_ref) = refs[-15:]
    c = CHUNK
    seq_len = nc * c
    chains = [(s, d) for s in range(ns) for d in range(2)]
    seg = seg_ref[...]

    for ch, (s, d) in enumerate(chains):
        st_ref[ch] = jnp.zeros((LANES, LANES), F32)
        if not zero_state:
            st_ref[ch, 0:HEAD_B, 0:HEAD_B] = s0_ref[s, d, 0]
            st_ref[ch, HEAD_B:LANES, HEAD_B:LANES] = s0_ref[s, d, 1]

    def block_of(bi):
        return pl.ds(pl.multiple_of(bi * block, block), block)

    def prepare(bi, carry):
        rows = block_of(bi)
        lora = lora_ref[rows, :]
        th, tl = _split(jnp.tanh(lora))
        w_hi, w_lo = wdec_ref[0], wdec_ref[1]
        dec = _dot(th, w_hi) + (_dot(th, w_lo) + _dot(tl, w_hi))
        w0 = w0_ref[...]
        lwf_s[rows, :] = -DECAY_SCALE * _sigmoid(w0[0:1] + dec[:, :LANES])
        lwb_s[rows, :] = -DECAY_SCALE * _sigmoid(w0[1:2] + dec[:, LANES:])
        a = _sigmoid(a0_ref[...] + _mm(lora, wa_ref[...]))
        k = k_ref[rows, :]
        kk = k * kkw_ref[...]
        kk = kk * lax.rsqrt(_mm(kk * kk, seg) + 1e-12)
        kk_s[rows, :] = kk
        beta_s[rows, :] = kk * a
        kmod_s[rows, :] = k * (1.0 + (a - 1.0) * ka_ref[...])
        return carry

    lax.fori_loop(0, ns * seq_len // block, prepare, 0, unroll=4)

    def rows_of(gi, j, s, d):
        cf = gi * group + j
        ci = cf if d == 0 else nc - 1 - cf
        return pl.ds(pl.multiple_of(s * seq_len + ci * c, c), c)

    def local_stage(gi, loc, dec):
        units = []
        for j in range(group):
            for s, d in chains:
                rows = rows_of(gi, j, s, d)
                units.append((r_ref[rows, :], kmod_s[rows, :], v_ref[rows, :], kk_s[rows, :], beta_s[rows, :],
                              (lwf_s, lwb_s)[d][rows, :], consts_ref[d], d == 1))
        results = yield from _rwkv_chunks_local(units)
        for u, (parts, decay) in enumerate(results):
            j, ch = divmod(u, len(chains))
            for idx, part in enumerate(parts):
                loc[ch, j, idx] = part
            dec[ch, j] = decay

    def state_stage(gi, loc, dec):
        states = [st_ref[ch] for ch in range(len(chains))]
        for j in range(group):
            parts = [[loc[ch, j, idx] for idx in range(RWKV_LOCAL_PARTS)] for ch in range(len(chains))]
            ys, states = yield from _rwkv_chunks_state(states, parts, [dec[ch, j] for ch in range(len(chains))])
            for y, (s, d) in zip(ys, chains):
                y_s[rows_of(gi, j, s, d), :] += y
        for ch, state in enumerate(states):
            st_ref[ch] = state

    y_s[...] = jnp.zeros(y_s.shape, F32)
    ng = nc // group
    _interleave(local_stage(0, loca_s, deca_s))

    def two_groups(p, carry):
        _interleave(local_stage(2 * p + 1, locb_s, decb_s), state_stage(2 * p, loca_s, deca_s))
        _interleave(local_stage(2 * p + 2, loca_s, deca_s), state_stage(2 * p + 1, locb_s, decb_s))
        return carry

    lax.fori_loop(0, ng // 2 - 1, two_groups, 0)
    _interleave(local_stage(ng - 1, locb_s, decb_s), state_stage(ng - 2, loca_s, deca_s))
    _interleave(state_stage(ng - 1, locb_s, decb_s))

    def finish(bi, carry):
        rows = block_of(bi)
        y = y_s[rows, :]
        inv_n = 1.0 / HEAD_B
        mu = _mm_exact_rhs(y, seg) * inv_n
        dy = y - mu
        var = _mm(dy * dy, seg) * inv_n
        yn = dy * lax.rsqrt(var + GN_EPS) * lnw_ref[...] + lnb_ref[...]
        v = v_ref[rows, :]
        bonus = _mm(r_ref[rows, :] * kmod_s[rows, :] * rk_ref[...], seg) * v
        g = _mm(_sigmoid(lora_ref[rows, :]), wg_ref[...])
        o_ref[rows, :] = (yn + bonus) * g
        return carry

    lax.fori_loop(0, ns * seq_len // block, finish, 0, unroll=4)
    for ch, (s, d) in enumerate(chains):
        sout_ref[s, d, 0] = st_ref[ch, 0:HEAD_B, 0:HEAD_B]
        sout_ref[s, d, 1] = st_ref[ch, HEAD_B:LANES, HEAD_B:LANES]


def _rwkv_mixer(u_b, p, s0, seq_len):
    t = u_b.shape[0]
    nb = t // seq_len
    nc = seq_len // CHUNK
    group = max(1, min(RWKV_UNITS // 2, nc // 4))
    ns = max(1, min(RWKV_UNITS // (2 * group), nb))
    assert nc % (2 * group) == 0 and nb % ns == 0
    rows = ns * seq_len
    block = min(MIXER_ROW_BLOCK, rows)
    pairs = N_HEADS_B // 2
    consts = jnp.asarray(_rwkv_consts(CHUNK), F32)
    lane = np.arange(LANES) // HEAD_B
    seg = jnp.asarray(lane[:, None] == lane[None, :], BF16)
    col = lambda part: pl.BlockSpec((rows, LANES), lambda b, h: (b, part * pairs + h))
    vec = lambda n: pl.BlockSpec((n, LANES), lambda b, h: (0, h))
    state_spec = pl.BlockSpec((ns, 2, 2, HEAD_B, HEAD_B), lambda b, h: (b, 0, h, 0, 0))
    scr = lambda: pltpu.VMEM((rows, LANES), F32)
    loc = lambda: pltpu.VMEM((2 * ns, group, RWKV_LOCAL_PARTS, LANES, LANES), F32)
    dec = lambda: pltpu.VMEM((2 * ns, group, 1, LANES), F32)
    state_in = ([], []) if s0 is None else ([state_spec], [s0])
    return pl.pallas_call(
        functools.partial(_rwkv_kernel, ns=ns, nc=nc, group=group, block=block, zero_state=s0 is None),
        grid=(nb // ns, pairs),
        in_specs=[col(0), col(1), col(2),
                  pl.BlockSpec((rows, LORA_COLS), lambda b, h: (b, 3 * WIDTH_B // LORA_COLS)),
                  vec(2), vec(1), vec(1), vec(1), vec(1), vec(1), vec(1),
                  pl.BlockSpec((None, 2, LORA_COLS, 2 * LANES), lambda b, h: (h, 0, 0, 0)),
                  pl.BlockSpec((LORA_COLS, LANES), lambda b, h: (0, h)),
                  pl.BlockSpec((LORA_COLS, LANES), lambda b, h: (0, h)),
                  *state_in[0], _full(consts.shape), _full(seg.shape)],
        out_specs=[pl.BlockSpec((rows, LANES), lambda b, h: (b, h)), state_spec],
        out_shape=[jax.ShapeDtypeStruct((t, WIDTH_B), F32), jax.ShapeDtypeStruct((nb, 2, N_HEADS_B, HEAD_B, HEAD_B), F32)],
        scratch_shapes=[scr(), scr(), scr(), scr(), scr(), scr(), loc(), dec(), loc(), dec(),
                        pltpu.VMEM((2 * ns, LANES, LANES), F32)],
        compiler_params=_params("parallel", "parallel"),
        name="rwkv7_scan",
    )(u_b, u_b, u_b, u_b, p["w0"], p["a0"], p["k_k"], p["k_a"], p["r_k"], p["ln_w"], p["ln_b"],
      p["w_dec"], p["w_a"], p["w_g"], *state_in[1], consts, seg)


def _mix_out_kernel(oa_ref, ob_ref, x_ref, mod_ref, woa_ref, wob_ref, nw_ref, wg_ref, wu_ref, cw_ref, cb_ref,
                    x1_ref, c_ref, u_ref, *, grid_conv):
    mod = mod_ref[...]
    gate1 = mod[:, 2 * D_MODEL:3 * D_MODEL]
    shift2, scale2 = mod[:, 3 * D_MODEL:4 * D_MODEL], mod[:, 4 * D_MODEL:5 * D_MODEL]
    mix = _dot(oa_ref[...].astype(BF16), woa_ref[...]) + _dot(ob_ref[...].astype(BF16), wob_ref[...])
    x1 = x_ref[...] + gate1 * mix
    x1_ref[...] = x1
    h = _modulated_norm(x1, nw_ref[...], scale2, shift2).astype(BF16)
    n = h.shape[0]
    width = GRID_W if grid_conv else n
    col = lax.broadcasted_iota(jnp.int32, (width, FFN_COL_BLOCK), 0)
    blocks = [slice(j * FFN_COL_BLOCK, (j + 1) * FFN_COL_BLOCK) for j in range(D_FF // FFN_COL_BLOCK)]
    g_next = _dot(h, wg_ref[:, blocks[0]])
    for j, cols in enumerate(blocks):
        g_all = g_next
        u_ref[:, cols] = _dot(h, wu_ref[:, cols]).astype(BF16)
        if j + 1 < len(blocks):
            g_next = _dot(h, wg_ref[:, blocks[j + 1]])
        cw = cw_ref[:, cols].astype(BF16)
        bias = cb_ref[:, cols].astype(BF16)
        for r0 in range(0, n, width):
            g = g_all[r0:r0 + width]
            left = jnp.where(col == 0, 0.0, pltpu.roll(g, 1, 0)).astype(BF16)
            right = jnp.where(col == width - 1, 0.0, pltpu.roll(g, width - 1, 0)).astype(BF16)
            g = g.astype(BF16)
            for slot, dr in enumerate((0, 1, 2) if grid_conv else (1,)):
                taps = cw[3 * dr:3 * dr + 1] * left + cw[3 * dr + 1:3 * dr + 2] * g + cw[3 * dr + 2:3 * dr + 3] * right
                c_ref[slot, r0:r0 + width, cols] = taps + bias if dr == 1 else taps


def _mix_out(o_a, o_b, x, mod, w_out_a, w_out_b, norm_w, w_gate, w_up, conv_w, conv_b, grid_conv, mod_of_tile):
    t = x.shape[0]
    nr = 3 if grid_conv else 1
    row = lambda n: pl.BlockSpec((ROW_TILE, n), lambda i: (i, 0))
    return pl.pallas_call(
        functools.partial(_mix_out_kernel, grid_conv=grid_conv),
        grid=(t // ROW_TILE,),
        in_specs=[row(WIDTH_A), row(WIDTH_B), row(D_MODEL),
                  pl.BlockSpec((None, 1, 6 * D_MODEL), lambda i: (mod_of_tile(i), 0, 0)),
                  _full(w_out_a.shape), _full(w_out_b.shape), _full((1, D_MODEL)), _full(w_gate.shape), _full(w_up.shape),
                  _full(conv_w.shape), _full(conv_b.shape)],
        out_specs=[row(D_MODEL), pl.BlockSpec((nr, ROW_TILE, D_FF), lambda i: (0, i, 0)), row(D_FF)],
        out_shape=[jax.ShapeDtypeStruct((t, D_MODEL), F32), jax.ShapeDtypeStruct((nr, t, D_FF), BF16),
                   jax.ShapeDtypeStruct((t, D_FF), BF16)],
        compiler_params=_params("parallel"),
        name="mix_out_ffn_in",
    )(o_a, o_b, x, mod, w_out_a, w_out_b, norm_w, w_gate, w_up, conv_w, conv_b)


def _gelu_tanh(x):
    half = 0.5 * x
    return half + half * jnp.tanh(x * (0.7978845608028654 + (0.7978845608028654 * 0.044715) * (x * x)))


def _ffn_out_kernel(*refs, seq_tiles, grid_conv):
    if grid_conv:
        c_ref, ca_ref, cb_ref, u_ref, x1_ref, mod_ref, wd_ref, fw_ref, y_ref = refs
        i = pl.program_id(0)
        top = (i % seq_tiles) == 0
        bottom = (i % seq_tiles) == seq_tiles - 1
        n = c_ref.shape[1]
        zero = jnp.zeros((), BF16)
        up_rows = jnp.concatenate([jnp.where(top, zero, ca_ref[...]), c_ref[0, :n - GRID_W, :]], axis=0)
        down_rows = jnp.concatenate([c_ref[2, GRID_W:, :], jnp.where(bottom, zero, cb_ref[...])], axis=0)
        gt = (c_ref[1] + up_rows + down_rows).astype(F32)
    else:
        c_ref, u_ref, x1_ref, mod_ref, wd_ref, fw_ref, y_ref = refs
        gt = c_ref[0].astype(F32)
    act = (_gelu_tanh(gt) * u_ref[...].astype(F32)).astype(BF16)
    mod = mod_ref[...]
    gate2 = mod[:, 5 * D_MODEL:6 * D_MODEL]
    x2 = x1_ref[...] + gate2 * _dot(act, wd_ref[...])
    y_ref[...] = x2 * lax.rsqrt(jnp.mean(x2 * x2, axis=-1, keepdims=True) + RMS_EPS) * fw_ref[...]


def _ffn_out(c, up, x1, mod, w_down, final_w, seq_tiles, grid_conv, mod_of_tile):
    t = x1.shape[0]
    nt = t // ROW_TILE
    per = ROW_TILE // GRID_W
    row = lambda n: pl.BlockSpec((ROW_TILE, n), lambda i: (i, 0))
    c_specs = [pl.BlockSpec((c.shape[0], ROW_TILE, D_FF), lambda i: (0, i, 0))]
    c_args = [c]
    if grid_conv:
        c_specs += [pl.BlockSpec((None, GRID_W, D_FF), lambda i: (0, jnp.maximum(i * per - 1, 0), 0)),
                    pl.BlockSpec((None, GRID_W, D_FF), lambda i: (2, jnp.minimum((i + 1) * per, nt * per - 1), 0))]
        c_args += [c, c]
    return pl.pallas_call(
        functools.partial(_ffn_out_kernel, seq_tiles=seq_tiles, grid_conv=grid_conv),
        grid=(nt,),
        in_specs=c_specs + [row(D_FF), row(D_MODEL),
                            pl.BlockSpec((None, 1, 6 * D_MODEL), lambda i: (mod_of_tile(i), 0, 0)),
                            _full(w_down.shape), _full((1, D_MODEL))],
        out_specs=row(D_MODEL),
        out_shape=jax.ShapeDtypeStruct((t, D_MODEL), F32),
        compiler_params=_params("parallel"),
        name="ffn_out",
    )(*c_args, up, x1, mod, w_down, final_w)


def _block(x, seq_len, mod, mod_of_tile, s_hgrn, s_rwkv, grid_conv, w):
    seq_tiles = seq_len // ROW_TILE
    u_a, u_b = _in_proj(x, mod, w["norm_mix_w"], w["w_in_a"], w["w_in_b"], w["rwkv_conv"], seq_tiles, mod_of_tile)
    o_a, s_h = _hgrn_mixer(u_a, w["hgrn_lb"], w["hgrn_norm_w"], s_hgrn, seq_len)
    o_b, s_r = _rwkv_mixer(u_b, w["rwkv"], s_rwkv, seq_len)
    x1, conv_sums, up = _mix_out(o_a, o_b, x, mod, w["w_out_a"], w["w_out_b"], w["norm_ffn_w"], w["ffn_w_gate"],
                                 w["ffn_w_up"], w["ffn_conv"], w["ffn_conv_b"], grid_conv, mod_of_tile)
    y = _ffn_out(conv_sums, up, x1, mod, w["ffn_w_down"], w["final_norm_w"], seq_tiles, grid_conv, mod_of_tile)
    return y, s_h, s_r


def _place_rows(w, start, total):
    return jnp.zeros((total, w.shape[1]), w.dtype).at[start:start + w.shape[0]].set(w)


def kernel(x_prompt, x_sample, state_hgrn, state_rwkv, c, c_ctx, ada_w, ada_b, norm_mix_w, w_in, hgrn_lb, hgrn_norm_w, rwkv_conv, rwkv_w0, rwkv_w2, rwkv_a0, rwkv_a2, rwkv_g2, rwkv_k_k, rwkv_k_a, rwkv_r_k, rwkv_ln_w, rwkv_ln_b, w_out, norm_ffn_w, ffn_w_gate, ffn_w_up, ffn_conv, ffn_conv_b, ffn_w_down, final_norm_w):
    assert w_in.shape[0] == 1, "one trunk layer"
    b_ctx, t_ctx, _ = x_prompt.shape
    b_lat, t_lat, _ = x_sample.shape
    row = lambda v: v.reshape(1, -1)
    pad_cols = P_B_PAD - P_B
    w_dec = jnp.concatenate([_place_rows(rwkv_w2[0, 0], 0, LORA_COLS), _place_rows(rwkv_w2[0, 1], LORA_W, LORA_COLS)], axis=1)
    pairs = N_HEADS_B // 2
    w_dec = w_dec.reshape(LORA_COLS, 2, pairs, LANES).transpose(2, 0, 1, 3).reshape(pairs, LORA_COLS, 2 * LANES)
    w_dec = jnp.stack(_split(w_dec), axis=1)
    weights = dict(
        norm_mix_w=row(norm_mix_w[0]),
        w_in_a=w_in[0, :, :P_A].astype(BF16),
        w_in_b=jnp.pad(w_in[0, :, P_A:], ((0, 0), (0, pad_cols))).astype(BF16),
        rwkv_conv=jnp.pad(rwkv_conv[0], ((0, 0), (0, pad_cols))),
        hgrn_lb=hgrn_lb,
        hgrn_norm_w=row(hgrn_norm_w[0]),
        rwkv=dict(w0=rwkv_w0[0], a0=row(rwkv_a0[0]), k_k=row(rwkv_k_k[0]), k_a=row(rwkv_k_a[0]), r_k=row(rwkv_r_k[0]),
                  ln_w=row(rwkv_ln_w[0]), ln_b=row(rwkv_ln_b[0]), w_dec=w_dec,
                  w_a=_place_rows(rwkv_a2[0], 2 * LORA_W, LORA_COLS),
                  w_g=_place_rows(rwkv_g2[0], 2 * LORA_W + LORA_A, LORA_COLS)),
        w_out_a=w_out[0, :WIDTH_A].astype(BF16),
        w_out_b=w_out[0, WIDTH_A:].astype(BF16),
        norm_ffn_w=row(norm_ffn_w[0]),
        ffn_w_gate=ffn_w_gate[0].astype(BF16),
        ffn_w_up=ffn_w_up[0].astype(BF16),
        ffn_conv=ffn_conv[0].reshape(9, D_FF),
        ffn_conv_b=row(ffn_conv_b[0]),
        ffn_w_down=ffn_w_down[0].astype(BF16),
        final_norm_w=row(final_norm_w),
    )
    cvec = jnp.concatenate([c_ctx[None, :], c, jnp.zeros((8 - 1 - b_lat, D_MODEL), F32)], axis=0)
    mod = _modulation(cvec, ada_w[0], row(ada_b[0])).reshape(8, 1, 6 * D_MODEL)

    yp, s_h, s_r = _block(x_prompt.reshape(b_ctx * t_ctx, D_MODEL), t_ctx, mod, lambda i: 0, None, None, False, weights)
    lat_tiles = t_lat // ROW_TILE
    ys, _, _ = _block(x_sample.reshape(b_lat * t_lat, D_MODEL), t_lat, mod, lambda i: 1 + i // lat_tiles,
                      state_hgrn[:, 0], state_rwkv[:, 0], True, weights)
    y_prompt = yp.reshape(b_ctx, t_ctx, D_MODEL)
    y_sample = ys.reshape(b_lat, t_lat, D_MODEL)
    new_state_hgrn = s_h[:, None]
    new_state_rwkv = s_r[:, None]
    return (y_prompt, y_sample, new_state_hgrn, new_state_rwkv)
```

```python
import functools

import numpy as np
import jax
import jax.numpy as jnp
from jax import lax
from jax.experimental import pallas as pl
from jax.experimental.pallas import tpu as pltpu

F32 = jnp.float32
BF16 = jnp.bfloat16

D_MODEL = 1024
GRID_W = 64
WIDTH_A = 512
HEAD_A = 128
N_HEADS_A = 4
WIDTH_B = 512
HEAD_B = 64
N_HEADS_B = 8
LORA_W = 32
LORA_A = 32
LORA_G = 96
D_FF = 2816
RMS_EPS = 1e-6
GN_EPS = 64e-5
DECAY_SCALE = 0.6065306597
P_A = 5 * WIDTH_A
P_B = 3 * WIDTH_B + 2 * LORA_W + LORA_A + LORA_G
LORA_COLS = 256
P_B_PAD = 3 * WIDTH_B + LORA_COLS

CHUNK = 64
LANES = 128
SUBLANES = 8
ROW_TILE = 256
FFN_OUT_ROWS = 512
FFN_COL_BLOCK = 256
VMEM_LIMIT = 56 * 1024 * 1024

NN = (((1,), (0,)), ((), ()))
NT = (((1,), (1,)), ((), ()))
TN = (((0,), (0,)), ((), ()))


def _dot(a, b, dims=NN):
    return lax.dot_general(a, b, dims, preferred_element_type=F32)


def _split(a):
    hi = a.astype(BF16)
    lo = (a - hi.astype(F32)).astype(BF16)
    return hi, lo


def _mm(a, b, dims=NN, passes=1):
    if passes == 1:
        return _dot(a.astype(BF16), b.astype(BF16), dims)
    ah, al = _split(a)
    bh, bl = _split(b)
    return _dot(ah, bh, dims) + (_dot(ah, bl, dims) + _dot(al, bh, dims))


def _mm_exact_lhs(m, x):
    n = x.shape[1]
    both = _dot(m, jnp.concatenate(_split(x), axis=1))
    return both[:, :n] + both[:, n:]


def _sigmoid(x):
    return 0.5 * jnp.tanh(0.5 * x) + 0.5


def _silu(x):
    return x * _sigmoid(x)


def _hgrn_level_consts(c, rev):
    t = np.arange(c)
    mats = [(t[None, :] <= t[:, None]).astype(np.float32)]
    masks = [np.eye(c, dtype=np.float32)]
    n = 1
    while n < c:
        blk, half = t // (2 * n), (t // n) % 2
        mid = blk * 2 * n + n
        m = np.zeros((c, c), np.float32)
        for row in range(c):
            if half[row] == 1:
                m[row, mid[row]:row + 1] = 1.0
            else:
                m[row, row + 1:mid[row]] = 1.0
        mats.append(m)
        masks.append(((blk[:, None] == blk[None, :]) & (half[:, None] == 1) & (half[None, :] == 0))
                     .astype(np.float32))
        n *= 2
    if rev:
        mats = [m[::-1, ::-1] for m in mats]
        masks = [m[::-1, ::-1] for m in masks]
    return np.concatenate(mats, 0), np.concatenate(masks, 0)


def _hgrn_consts(c):
    mf, kf = _hgrn_level_consts(c, False)
    mb, kb = _hgrn_level_consts(c, True)
    mat_levels = 1 + int(np.log2(SUBLANES))
    return np.stack([mf, mb])[:, :mat_levels * c], np.stack([kf, kb])


def _rwkv_consts(c):
    t = np.arange(c)
    out = []
    for rev in (False, True):
        incl = (t[None, :] <= t[:, None]) if not rev else (t[None, :] >= t[:, None])
        strict = (t[None, :] < t[:, None]) if not rev else (t[None, :] > t[:, None])
        z = np.zeros((c, c), bool)
        cum = np.block([[incl, z], [z, z]])
        out.append(np.stack([cum, np.block([[strict, z], [z, strict]]), np.block([[incl, z], [z, incl]])]))
    return np.stack(out).astype(np.float32)


def _each(fn, *cols):
    return [fn(*args) for args in zip(*cols)]


def _interleave(*gens):
    results = [None] * len(gens)
    live = dict(enumerate(gens))
    while live:
        for i in list(live):
            try:
                next(live[i])
            except StopIteration as stop:
                results[i] = stop.value
                del live[i]
    return results


def _pair_exponent(b, n, rev):
    parts = []
    for base in range(0, b.shape[0], 2 * n):
        first, second = b[base:base + n], b[base + n:base + 2 * n]
        if rev:
            r = jnp.broadcast_to(b[base + n:base + n + 1], first.shape)
            parts += [first - r, r - second]
        else:
            r = jnp.broadcast_to(b[base + n - 1:base + n], first.shape)
            parts += [r - first, second - r]
    return jnp.concatenate(parts, axis=0)


def _hgrn_chunks_local(units):
    q, k, v, g, m, masks, rev = (list(col) for col in zip(*units))
    c = q[0].shape[0]
    levels = masks[0].shape[0] // c
    mat_levels = m[0].shape[0] // c
    e_all = _each(_mm_exact_lhs, m, g)
    yield
    b = [x[:c] for x in e_all]
    edge = _each(lambda x, rv: x[0:1] if rv else x[c - 1:c], b, rev)
    kv = _each(lambda vv, kk, ed, bb: _mm(vv, kk * jnp.exp(ed - bb), TN), v, k, edge, b)
    yield
    sc = _each(lambda mk, a, kk: mk[:c] * _mm(a, kk, NT), masks, q, k)
    yield
    for l in range(1, levels):
        if l < mat_levels:
            e = _each(lambda x: jnp.exp(x[l * c:(l + 1) * c]), e_all)
        else:
            e = _each(lambda bb, rv: jnp.exp(_pair_exponent(bb, 1 << (l - 1), rv)), b, rev)
        sc = _each(lambda s, mk, a, kk, ee: s + mk[l * c:(l + 1) * c] * _mm(a * ee, kk * ee, NT), sc, masks, q, k, e)
        yield
    intra = _each(_mm, sc, v)
    yield
    return [(ii, qq * jnp.exp(bb), kvv, jnp.exp(ed)) for ii, qq, bb, kvv, ed in zip(intra, q, b, kv, edge)]


def _hgrn_chunk_state(st, part):
    intra, qb, kv, decay = part
    return intra + _mm(qb, st, NT), st * decay + kv


def _stack_heads(x, lane_head):
    return jnp.concatenate([jnp.where(lane_head == 0, x, 0.0), jnp.where(lane_head == 1, x, 0.0)], axis=0)


RWKV_PASSES_SCORE = 1
RWKV_PASSES_SOLVE = 1
RWKV_PASSES_STATE = 1
RWKV_LOCAL_PARTS = 7
RWKV_UNITS = 8
HGRN_UNITS = 16
MIXER_ROW_BLOCK = 256


def _rwkv_chunks_local(units):
    c = units[0][0].shape[0]
    lane_head = lax.broadcasted_iota(jnp.int32, (c, LANES), 1) // HEAD_B
    st = lambda x: _stack_heads(x, lane_head)
    r, k, v, kk, beta, lw, consts, rev = (list(col) for col in zip(*units))
    strict = [cs[1] for cs in consts]
    incl = [cs[2] for cs in consts]
    g = _each(lambda cs, x: _mm_exact_lhs(cs[0][:c, :c].astype(BF16), x), consts, lw)
    edge = _each(lambda x, rv: x[0:1] if rv else x[c - 1:c], g, rev)
    yield
    kg = _each(lambda x, gg, l: st(x * jnp.exp(gg - l)), kk, g, lw)
    rg = _each(lambda x, gg: st(x * jnp.exp(gg)), r, g)
    e_out = _each(lambda gg: jnp.exp(-gg), g)
    bi = _each(lambda x, e: st(x * e), beta, e_out)
    ki = _each(lambda x, e: st(x * e), k, e_out)
    vs = _each(st, v)
    a_all = _each(lambda a, b, cc, d: _mm(jnp.concatenate([a, b], 0), jnp.concatenate([cc, d], 0), NT, RWKV_PASSES_SCORE),
                  kg, rg, bi, ki)
    yield
    n = _each(lambda m, a: m * a[:2 * c, :2 * c], strict, a_all)
    a_ak = _each(lambda m, a: m * a[:2 * c, 2 * c:], strict, a_all)
    a_rb = _each(lambda m, a: m * a[2 * c:, :2 * c], incl, a_all)
    a_rk = _each(lambda m, a: m * a[2 * c:, 2 * c:], incl, a_all)
    av = _each(lambda a, b, x: _mm(jnp.concatenate([a, b], 0), x, passes=RWKV_PASSES_SCORE), a_ak, a_rk, vs)
    yield
    eye = (lax.broadcasted_iota(jnp.int32, (2 * c, 2 * c), 0) == lax.broadcasted_iota(jnp.int32, (2 * c, 2 * c), 1)).astype(F32)
    tinv = _each(lambda a: eye - a, n)
    power = _each(lambda a: _mm(a, a, passes=RWKV_PASSES_SOLVE), n)
    yield
    span = 4
    while span < c:
        both = _each(lambda p, t: _mm(jnp.concatenate([p, t], axis=0), p, passes=RWKV_PASSES_SOLVE), power, tinv)
        power = [b[:2 * c] for b in both]
        tinv = _each(lambda t, b: t + b[2 * c:], tinv, both)
        span *= 2
        yield
    tinv = _each(lambda p, t: t + _mm(t, p, passes=RWKV_PASSES_SOLVE), power, tinv)
    yield
    x = _each(lambda t, a, b: _mm(t, jnp.concatenate([a, b[:2 * c]], axis=1), passes=RWKV_PASSES_SOLVE), tinv, kg, av)
    yield
    e_edge = _each(lambda ed, gg: jnp.exp(ed - gg), edge, g)
    kv = _each(lambda a, b, e: _mm(a, st(b * e), TN, RWKV_PASSES_STATE), vs, k, e_edge)
    yield
    bgc = _each(lambda b, e: st(b * e), beta, e_edge)
    return [((xx[:, :LANES], rr, xx[:, LANES:], ab, a[2 * c:], bg, kvv), jnp.exp(ed))
            for xx, rr, ab, a, bg, kvv, ed in zip(x, rg, a_rb, av, bgc, kv, edge)]


def _rwkv_chunks_state(states, parts, decays):
    w1, rg, x2, a_rb, y0, bgc, kv = (list(col) for col in zip(*parts))
    c2 = w1[0].shape[0]
    hs = _each(lambda a, b, s: _mm(jnp.concatenate([a, b], 0), s, NT, RWKV_PASSES_STATE), w1, rg, states)
    yield
    u = _each(lambda h, x: -(h[:c2] + x), hs, x2)
    s = _each(lambda s_, d, uu, b, kv_: s_ * d + _mm(uu, b, TN, RWKV_PASSES_STATE) + kv_, states, decays, u, bgc, kv)
    yield
    y = _each(lambda h, y_, a, uu: h[c2:] + y_ + _mm(a, uu, passes=RWKV_PASSES_STATE), hs, y0, a_rb, u)
    yield
    return [yy[:c2 // 2] + yy[c2 // 2:] for yy in y], s


def _params(*semantics):
    return pltpu.CompilerParams(dimension_semantics=semantics, vmem_limit_bytes=VMEM_LIMIT)


def _full(shape):
    return pl.BlockSpec(shape, lambda *_: (0,) * len(shape))


def _modulated_norm(x, norm_w, scale, shift):
    y = x * lax.rsqrt(jnp.mean(x * x, axis=-1, keepdims=True) + RMS_EPS)
    return (y * norm_w) * (1.0 + scale) + shift


def _mod_kernel(c_ref, w_ref, b_ref, o_ref):
    o_ref[...] = _mm(_silu(c_ref[...]), w_ref[...], passes=3) + b_ref[...]


def _modulation(cvec, ada_w, ada_b):
    n = ada_w.shape[2]
    tn = n // 4
    return pl.pallas_call(
        _mod_kernel,
        grid=(n // tn,),
        in_specs=[_full(cvec.shape), pl.BlockSpec((None, D_MODEL, tn), lambda j: (0, 0, j)),
                  pl.BlockSpec((1, tn), lambda j: (0, j))],
        out_specs=pl.BlockSpec((cvec.shape[0], tn), lambda j: (0, j)),
        out_shape=jax.ShapeDtypeStruct((cvec.shape[0], n), F32),
        compiler_params=_params("arbitrary"),
        name="modulation",
    )(cvec, ada_w, ada_b)


def _in_proj_kernel(x_ref, xp_ref, xn_ref, mod_ref, nw_ref, wa_ref, wb_ref, cw_ref, ua_ref, ub_ref, *, seq_tiles):
    i = pl.program_id(0)
    mod = mod_ref[...]
    shift, scale = mod[:, 0:D_MODEL], mod[:, D_MODEL:2 * D_MODEL]
    nw = nw_ref[...]
    h = _modulated_norm(x_ref[...], nw, scale, shift).astype(BF16)
    ua_ref[...] = _dot(h, wa_ref[...])
    halo = jnp.concatenate([xp_ref[...], xn_ref[...]], axis=0)
    hh = _modulated_norm(halo, nw, scale, shift).astype(BF16)
    ub_all = _dot(jnp.concatenate([h, hh], axis=0), wb_ref[...])
    n = h.shape[0]
    ub, ubh = ub_all[:n], ub_all[n:]
    first = (i % seq_tiles) == 0
    last = (i % seq_tiles) == seq_tiles - 1
    prev_row = jnp.where(first, 0.0, ubh[7:8])
    next_row = jnp.where(last, 0.0, ubh[8:9])
    rows = lax.broadcasted_iota(jnp.int32, ub.shape, 0)
    below = jnp.where(rows == 0, prev_row, pltpu.roll(ub, 1, 0))
    above = jnp.where(rows == n - 1, next_row, pltpu.roll(ub, n - 1, 0))
    cw = cw_ref[...]
    ub_ref[...] = cw[0:1] * below + cw[1:2] * ub + cw[2:3] * above


def _in_proj(x, mod, norm_w, w_a, w_b, conv_w, seq_tiles, mod_of_tile):
    t = x.shape[0]
    nt = t // ROW_TILE
    sub = ROW_TILE // 8
    return pl.pallas_call(
        functools.partial(_in_proj_kernel, seq_tiles=seq_tiles),
        grid=(nt,),
        in_specs=[
            pl.BlockSpec((ROW_TILE, D_MODEL), lambda i: (i, 0)),
            pl.BlockSpec((8, D_MODEL), lambda i: (jnp.maximum(i * sub - 1, 0), 0)),
            pl.BlockSpec((8, D_MODEL), lambda i: (jnp.minimum((i + 1) * sub, nt * sub - 1), 0)),
            pl.BlockSpec((None, 1, 6 * D_MODEL), lambda i: (mod_of_tile(i), 0, 0)),
            _full((1, D_MODEL)), _full(w_a.shape), _full(w_b.shape), _full(conv_w.shape),
        ],
        out_specs=[pl.BlockSpec((ROW_TILE, P_A), lambda i: (i, 0)), pl.BlockSpec((ROW_TILE, P_B_PAD), lambda i: (i, 0))],
        out_shape=[jax.ShapeDtypeStruct((t, P_A), F32), jax.ShapeDtypeStruct((t, P_B_PAD), F32)],
        compiler_params=_params("parallel"),
        name="in_proj",
    )(x, x, x, mod, norm_w, w_a, w_b, conv_w)


def _hgrn_kernel(*refs, ns, nc, group, block, zero_state):
    q_ref, i_ref, zf_ref, zb_ref, g_ref, lb_ref, nw_ref = refs[:7]
    s0_ref = None if zero_state else refs[7]
    m_ref, mask_ref, o_ref, sout_ref, acc_s, st_ref = refs[-6:]
    c = CHUNK
    seq_len = nc * c
    chains = [(s, d) for s in range(ns) for d in range(2)]
    p0, p1 = lb_ref[0], lb_ref[1]
    mx = jnp.maximum(p0, p1)
    e0, e1 = jnp.exp(p0 - mx), jnp.exp(p1 - mx)
    lb = e0 / (e0 + e1)
    for ch, (s, d) in enumerate(chains):
        st_ref[ch] = jnp.zeros((HEAD_A, HEAD_A), F32) if zero_state else s0_ref[s, d].T

    def rows_of(gi, j, s, d):
        cf = gi * group + j
        ci = cf if d == 0 else nc - 1 - cf
        return pl.ds(pl.multiple_of(s * seq_len + ci * c, c), c)

    def scan_group(gi, carry):
        units = []
        for j in range(group):
            for s, d in chains:
                rows = rows_of(gi, j, s, d)
                lo = lb[d:d + 1]
                f = lo + (1.0 - lo) * _sigmoid((zf_ref, zb_ref)[d][rows, :])
                units.append((_silu(q_ref[rows, :]), 1.0 - f, i_ref[rows, :], jnp.log(f), m_ref[d], mask_ref[d], d == 1))
        parts, = _interleave(_hgrn_chunks_local(units))
        st = [st_ref[ch] for ch in range(len(chains))]
        for j in range(group):
            for ch, (s, d) in enumerate(chains):
                o, st[ch] = _hgrn_chunk_state(st[ch], parts[j * len(chains) + ch])
                acc_s[rows_of(gi, j, s, d), :] += o
        for ch, state in enumerate(st):
            st_ref[ch] = state
        return carry

    acc_s[...] = jnp.zeros(acc_s.shape, F32)
    lax.fori_loop(0, nc // group, scan_group, 0)

    def finish(bi, carry):
        rows = pl.ds(pl.multiple_of(bi * block, block), block)
        o = acc_s[rows, :]
        o = o * lax.rsqrt(jnp.mean(o * o, axis=-1, keepdims=True) + RMS_EPS) * nw_ref[...]
        o_ref[rows, :] = o * _silu(g_ref[rows, :])
        return carry

    lax.fori_loop(0, ns * seq_len // block, finish, 0, unroll=2 if ns * seq_len // block % 2 == 0 else 1)
    for ch, (s, d) in enumerate(chains):
        sout_ref[s, d] = st_ref[ch].T


def _hgrn_mixer(u_a, lb_raw, norm_w, s0, seq_len):
    t = u_a.shape[0]
    nb = t // seq_len
    nc = seq_len // CHUNK
    group = min(HGRN_UNITS // 2, nc)
    ns = max(1, min(HGRN_UNITS // (2 * group), nb))
    assert nc % group == 0 and nb % ns == 0
    rows = ns * seq_len
    mats, masks = _hgrn_consts(CHUNK)
    mats = jnp.asarray(mats, BF16)
    masks = jnp.asarray(masks, F32)
    col = lambda part: pl.BlockSpec((rows, HEAD_A), lambda b, h: (b, part * N_HEADS_A + h))
    state_spec = pl.BlockSpec((ns, 2, None, HEAD_A, HEAD_A), lambda b, h: (b, 0, h, 0, 0))
    state_in = ([], []) if s0 is None else ([state_spec], [s0])
    return pl.pallas_call(
        functools.partial(_hgrn_kernel, ns=ns, nc=nc, group=group, block=min(MIXER_ROW_BLOCK, rows),
                          zero_state=s0 is None),
        grid=(nb // ns, N_HEADS_A),
        in_specs=[col(0), col(1), col(2), col(3), col(4),
                  pl.BlockSpec((2, 2, HEAD_A), lambda b, h: (0, 0, h)),
                  pl.BlockSpec((1, HEAD_A), lambda b, h: (0, h)),
                  *state_in[0], _full(mats.shape), _full(masks.shape)],
        out_specs=[pl.BlockSpec((rows, HEAD_A), lambda b, h: (b, h)), state_spec],
        out_shape=[jax.ShapeDtypeStruct((t, WIDTH_A), F32), jax.ShapeDtypeStruct((nb, 2, N_HEADS_A, HEAD_A, HEAD_A), F32)],
        scratch_shapes=[pltpu.VMEM((rows, HEAD_A), F32), pltpu.VMEM((2 * ns, HEAD_A, HEAD_A), F32)],
        compiler_params=_params("parallel", "parallel"),
        name="hgrn2_scan",
    )(u_a, u_a, u_a, u_a, u_a, lb_raw, norm_w, *state_in[1], mats, masks)


def _mm_exact_rhs(x, m):
    n = x.shape[0]
    both = _dot(jnp.concatenate(_split(x), axis=0), m)
    return both[:n] + both[n:]


def _rwkv_kernel(*refs, ns, nc, group, block, zero_state):
    (r_ref, k_ref, v_ref, lora_ref, w0_ref, a0_ref, kkw_ref, ka_ref, rk_ref, lnw_ref, lnb_ref,
     wdec_ref, wa_ref, wg_ref) = refs[:14]
    s0_ref = None if zero_state else refs[14]
    (consts_ref, seg_ref, o_ref, sout_ref, lwf_s, lwb_s, kk_s, beta_s, kmod_s, y_s,
     loca_s, deca_s, locb_s, decb_s, st_ref) = refs[-15:]
    c = CHUNK
    seq_len = nc * c
    chains = [(s, d) for s in range(ns) for d in range(2)]
    seg = seg_ref[...]

    for ch, (s, d) in enumerate(chains):
        st_ref[ch] = jnp.zeros((LANES, LANES), F32)
        if not zero_state:
            st_ref[ch, 0:HEAD_B, 0:HEAD_B] = s0_ref[s, d, 0]
            st_ref[ch, HEAD_B:LANES, HEAD_B:LANES] = s0_ref[s, d, 1]

    def block_of(bi):
        return pl.ds(bi * block, block)

    edge_chunks = list(range(group)) + list(range(nc - group, nc))
    edge_blocks = sorted({(s * seq_len + ci * c) // block for s in range(ns) for ci in edge_chunks})
    inner_blocks = [bi for bi in range(ns * seq_len // block) if bi not in edge_blocks]

    def one_per_stage(fn, blocks):
        for bi in blocks:
            fn(bi)
            yield

    def prepare(bi):
        rows = block_of(bi)
        lora = lora_ref[rows, :]
        th, tl = _split(jnp.tanh(lora))
        w_hi, w_lo = wdec_ref[0], wdec_ref[1]
        dec = _dot(th, w_hi) + (_dot(th, w_lo) + _dot(tl, w_hi))
        w0 = w0_ref[...]
        lwf_s[rows, :] = -DECAY_SCALE * _sigmoid(w0[0:1] + dec[:, :LANES])
        lwb_s[rows, :] = -DECAY_SCALE * _sigmoid(w0[1:2] + dec[:, LANES:])
        a = _sigmoid(a0_ref[...] + _mm(lora, wa_ref[...]))
        k = k_ref[rows, :]
        kk = k * kkw_ref[...]
        kk = kk * lax.rsqrt(_mm(kk * kk, seg) + 1e-12)
        kk_s[rows, :] = kk
        beta_s[rows, :] = kk * a
        kmod_s[rows, :] = k * (1.0 + (a - 1.0) * ka_ref[...])

    for bi in edge_blocks:
        prepare(bi)

    def rows_of(gi, j, s, d):
        cf = gi * group + j
        ci = cf if d == 0 else nc - 1 - cf
        return pl.ds(pl.multiple_of(s * seq_len + ci * c, c), c)

    def local_stage(gi, loc, dec):
        units = []
        for j in range(group):
            for s, d in chains:
                rows = rows_of(gi, j, s, d)
                units.append((r_ref[rows, :], kmod_s[rows, :], v_ref[rows, :], kk_s[rows, :], beta_s[rows, :],
                              (lwf_s, lwb_s)[d][rows, :], consts_ref[d], d == 1))
        results = yield from _rwkv_chunks_local(units)
        for u, (parts, decay) in enumerate(results):
            j, ch = divmod(u, len(chains))
            for idx, part in enumerate(parts):
                loc[ch, j, idx] = part
            dec[ch, j] = decay

    def state_stage(gi, loc, dec):
        states = [st_ref[ch] for ch in range(len(chains))]
        for j in range(group):
            parts = [[loc[ch, j, idx] for idx in range(RWKV_LOCAL_PARTS)] for ch in range(len(chains))]
            ys, states = yield from _rwkv_chunks_state(states, parts, [dec[ch, j] for ch in range(len(chains))])
            for y, (s, d) in zip(ys, chains):
                y_s[rows_of(gi, j, s, d), :] += y
        for ch, state in enumerate(states):
            st_ref[ch] = state

    y_s[...] = jnp.zeros(y_s.shape, F32)
    ng = nc // group
    _interleave(local_stage(0, loca_s, deca_s), one_per_stage(prepare, inner_blocks))

    def two_groups(p, carry):
        _interleave(local_stage(2 * p + 1, locb_s, decb_s), state_stage(2 * p, loca_s, deca_s))
        _interleave(local_stage(2 * p + 2, loca_s, deca_s), state_stage(2 * p + 1, locb_s, decb_s))
        return carry

    lax.fori_loop(0, ng // 2 - 1, two_groups, 0)
    _interleave(local_stage(ng - 1, locb_s, decb_s), state_stage(ng - 2, loca_s, deca_s))
    def finish(bi):
        rows = block_of(bi)
        y = y_s[rows, :]
        inv_n = 1.0 / HEAD_B
        mu = _mm_exact_rhs(y, seg) * inv_n
        dy = y - mu
        var = _mm(dy * dy, seg) * inv_n
        yn = dy * lax.rsqrt(var + GN_EPS) * lnw_ref[...] + lnb_ref[...]
        v = v_ref[rows, :]
        bonus = _mm(r_ref[rows, :] * kmod_s[rows, :] * rk_ref[...], seg) * v
        g = _mm(_sigmoid(lora_ref[rows, :]), wg_ref[...])
        o_ref[rows, :] = (yn + bonus) * g

    _interleave(state_stage(ng - 1, locb_s, decb_s), one_per_stage(finish, inner_blocks))
    for bi in edge_blocks:
        finish(bi)
    for ch, (s, d) in enumerate(chains):
        sout_ref[s, d, 0] = st_ref[ch, 0:HEAD_B, 0:HEAD_B]
        sout_ref[s, d, 1] = st_ref[ch, HEAD_B:LANES, HEAD_B:LANES]


def _rwkv_mixer(u_b, p, s0, seq_len):
    t = u_b.shape[0]
    nb = t // seq_len
    nc = seq_len // CHUNK
    group = max(1, min(RWKV_UNITS // 2, nc // 4))
    ns = max(1, min(RWKV_UNITS // (2 * group), nb))
    assert nc % (2 * group) == 0 and nb % ns == 0
    rows = ns * seq_len
    block = min(MIXER_ROW_BLOCK, rows)
    pairs = N_HEADS_B // 2
    consts = jnp.asarray(_rwkv_consts(CHUNK), F32)
    lane = np.arange(LANES) // HEAD_B
    seg = jnp.asarray(lane[:, None] == lane[None, :], BF16)
    col = lambda part: pl.BlockSpec((rows, LANES), lambda b, h: (b, part * pairs + h))
    vec = lambda n: pl.BlockSpec((n, LANES), lambda b, h: (0, h))
    state_spec = pl.BlockSpec((ns, 2, 2, HEAD_B, HEAD_B), lambda b, h: (b, 0, h, 0, 0))
    scr = lambda: pltpu.VMEM((rows, LANES), F32)
    loc = lambda: pltpu.VMEM((2 * ns, group, RWKV_LOCAL_PARTS, LANES, LANES), F32)
    dec = lambda: pltpu.VMEM((2 * ns, group, 1, LANES), F32)
    state_in = ([], []) if s0 is None else ([state_spec], [s0])
    return pl.pallas_call(
        functools.partial(_rwkv_kernel, ns=ns, nc=nc, group=group, block=block, zero_state=s0 is None),
        grid=(nb // ns, pairs),
        in_specs=[col(0), col(1), col(2),
                  pl.BlockSpec((rows, LORA_COLS), lambda b, h: (b, 3 * WIDTH_B // LORA_COLS)),
                  vec(2), vec(1), vec(1), vec(1), vec(1), vec(1), vec(1),
                  pl.BlockSpec((None, 2, LORA_COLS, 2 * LANES), lambda b, h: (h, 0, 0, 0)),
                  pl.BlockSpec((LORA_COLS, LANES), lambda b, h: (0, h)),
                  pl.BlockSpec((LORA_COLS, LANES), lambda b, h: (0, h)),
                  *state_in[0], _full(consts.shape), _full(seg.shape)],
        out_specs=[pl.BlockSpec((rows, LANES), lambda b, h: (b, h)), state_spec],
        out_shape=[jax.ShapeDtypeStruct((t, WIDTH_B), F32), jax.ShapeDtypeStruct((nb, 2, N_HEADS_B, HEAD_B, HEAD_B), F32)],
        scratch_shapes=[scr(), scr(), scr(), scr(), scr(), scr(), loc(), dec(), loc(), dec(),
                        pltpu.VMEM((2 * ns, LANES, LANES), F32)],
        compiler_params=_params("parallel", "parallel"),
        name="rwkv7_scan",
    )(u_b, u_b, u_b, u_b, p["w0"], p["a0"], p["k_k"], p["k_a"], p["r_k"], p["ln_w"], p["ln_b"],
      p["w_dec"], p["w_a"], p["w_g"], *state_in[1], consts, seg)


def _mix_out_kernel(oa_ref, ob_ref, x_ref, mod_ref, woa_ref, wob_ref, nw_ref, wg_ref, wu_ref, cw_ref, cb_ref,
                    x1_ref, c_ref, u_ref, *, grid_conv):
    mod = mod_ref[...]
    gate1 = mod[:, 2 * D_MODEL:3 * D_MODEL]
    shift2, scale2 = mod[:, 3 * D_MODEL:4 * D_MODEL], mod[:, 4 * D_MODEL:5 * D_MODEL]
    mix = _dot(oa_ref[...].astype(BF16), woa_ref[...]) + _dot(ob_ref[...].astype(BF16), wob_ref[...])
    x1 = x_ref[...] + gate1 * mix
    x1_ref[...] = x1
    h = _modulated_norm(x1, nw_ref[...], scale2, shift2).astype(BF16)
    n = h.shape[0]
    width = GRID_W if grid_conv else n
    col = lax.broadcasted_iota(jnp.int32, (width, FFN_COL_BLOCK), 0)
    blocks = [slice(j * FFN_COL_BLOCK, (j + 1) * FFN_COL_BLOCK) for j in range(D_FF // FFN_COL_BLOCK)]
    g_next = _dot(h, wg_ref[:, blocks[0]])
    for j, cols in enumerate(blocks):
        g_all = g_next
        u_ref[:, cols] = _dot(h, wu_ref[:, cols]).astype(BF16)
        if j + 1 < len(blocks):
            g_next = _dot(h, wg_ref[:, blocks[j + 1]])
        cw = cw_ref[:, cols].astype(BF16)
        bias = cb_ref[:, cols].astype(BF16)
        for r0 in range(0, n, width):
            g = g_all[r0:r0 + width]
            left = jnp.where(col == 0, 0.0, pltpu.roll(g, 1, 0)).astype(BF16)
            right = jnp.where(col == width - 1, 0.0, pltpu.roll(g, width - 1, 0)).astype(BF16)
            g = g.astype(BF16)
            for slot, dr in enumerate((0, 1, 2) if grid_conv else (1,)):
                taps = cw[3 * dr:3 * dr + 1] * left + cw[3 * dr + 1:3 * dr + 2] * g + cw[3 * dr + 2:3 * dr + 3] * right
                c_ref[slot, r0:r0 + width, cols] = taps + bias if dr == 1 else taps


def _mix_out(o_a, o_b, x, mod, w_out_a, w_out_b, norm_w, w_gate, w_up, conv_w, conv_b, grid_conv, mod_of_tile):
    t = x.shape[0]
    nr = 3 if grid_conv else 1
    row = lambda n: pl.BlockSpec((ROW_TILE, n), lambda i: (i, 0))
    return pl.pallas_call(
        functools.partial(_mix_out_kernel, grid_conv=grid_conv),
        grid=(t // ROW_TILE,),
        in_specs=[row(WIDTH_A), row(WIDTH_B), row(D_MODEL),
                  pl.BlockSpec((None, 1, 6 * D_MODEL), lambda i: (mod_of_tile(i), 0, 0)),
                  _full(w_out_a.shape), _full(w_out_b.shape), _full((1, D_MODEL)), _full(w_gate.shape), _full(w_up.shape),
                  _full(conv_w.shape), _full(conv_b.shape)],
        out_specs=[row(D_MODEL), pl.BlockSpec((nr, ROW_TILE, D_FF), lambda i: (0, i, 0)), row(D_FF)],
        out_shape=[jax.ShapeDtypeStruct((t, D_MODEL), F32), jax.ShapeDtypeStruct((nr, t, D_FF), BF16),
                   jax.ShapeDtypeStruct((t, D_FF), BF16)],
        compiler_params=_params("parallel"),
        name="mix_out_ffn_in",
    )(o_a, o_b, x, mod, w_out_a, w_out_b, norm_w, w_gate, w_up, conv_w, conv_b)


def _gelu_tanh(x):
    half = 0.5 * x
    return half + half * jnp.tanh(x * (0.7978845608028654 + (0.7978845608028654 * 0.044715) * (x * x)))


def _ffn_out_kernel(*refs, seq_tiles, grid_conv):
    if grid_conv:
        c_ref, ca_ref, cb_ref, u_ref, x1_ref, mod_ref, wd_ref, fw_ref, y_ref = refs
        i = pl.program_id(0)
        top = (i % seq_tiles) == 0
        bottom = (i % seq_tiles) == seq_tiles - 1
        n = c_ref.shape[1]
        zero = jnp.zeros((), BF16)
        up_rows = jnp.concatenate([jnp.where(top, zero, ca_ref[...]), c_ref[0, :n - GRID_W, :]], axis=0)
        down_rows = jnp.concatenate([c_ref[2, GRID_W:, :], jnp.where(bottom, zero, cb_ref[...])], axis=0)
        gt = (c_ref[1] + up_rows + down_rows).astype(F32)
    else:
        c_ref, u_ref, x1_ref, mod_ref, wd_ref, fw_ref, y_ref = refs
        gt = c_ref[0].astype(F32)
    act = (_gelu_tanh(gt) * u_ref[...].astype(F32)).astype(BF16)
    mod = mod_ref[...]
    gate2 = mod[:, 5 * D_MODEL:6 * D_MODEL]
    x2 = x1_ref[...] + gate2 * _dot(act, wd_ref[...])
    y_ref[...] = x2 * lax.rsqrt(jnp.mean(x2 * x2, axis=-1, keepdims=True) + RMS_EPS) * fw_ref[...]


def _ffn_out(c, up, x1, mod, w_down, final_w, seq_tiles, grid_conv, mod_of_tile):
    t = x1.shape[0]
    tile = FFN_OUT_ROWS
    scale = tile // ROW_TILE
    nt = t // tile
    per = tile // GRID_W
    row = lambda n: pl.BlockSpec((tile, n), lambda i: (i, 0))
    c_specs = [pl.BlockSpec((c.shape[0], tile, D_FF), lambda i: (0, i, 0))]
    c_args = [c]
    if grid_conv:
        c_specs += [pl.BlockSpec((None, GRID_W, D_FF), lambda i: (0, jnp.maximum(i * per - 1, 0), 0)),
                    pl.BlockSpec((None, GRID_W, D_FF), lambda i: (2, jnp.minimum((i + 1) * per, nt * per - 1), 0))]
        c_args += [c, c]
    return pl.pallas_call(
        functools.partial(_ffn_out_kernel, seq_tiles=seq_tiles // scale, grid_conv=grid_conv),
        grid=(nt,),
        in_specs=c_specs + [row(D_FF), row(D_MODEL),
                            pl.BlockSpec((None, 1, 6 * D_MODEL), lambda i: (mod_of_tile(i * scale), 0, 0)),
                            _full(w_down.shape), _full((1, D_MODEL))],
        out_specs=row(D_MODEL),
        out_shape=jax.ShapeDtypeStruct((t, D_MODEL), F32),
        compiler_params=_params("parallel"),
        name="ffn_out",
    )(*c_args, up, x1, mod, w_down, final_w)


def _block(x, seq_len, mod, mod_of_tile, s_hgrn, s_rwkv, grid_conv, w):
    seq_tiles = seq_len // ROW_TILE
    u_a, u_b = _in_proj(x, mod, w["norm_mix_w"], w["w_in_a"], w["w_in_b"], w["rwkv_conv"], seq_tiles, mod_of_tile)
    o_a, s_h = _hgrn_mixer(u_a, w["hgrn_lb"], w["hgrn_norm_w"], s_hgrn, seq_len)
    o_b, s_r = _rwkv_mixer(u_b, w["rwkv"], s_rwkv, seq_len)
    x1, conv_sums, up = _mix_out(o_a, o_b, x, mod, w["w_out_a"], w["w_out_b"], w["norm_ffn_w"], w["ffn_w_gate"],
                                 w["ffn_w_up"], w["ffn_conv"], w["ffn_conv_b"], grid_conv, mod_of_tile)
    y = _ffn_out(conv_sums, up, x1, mod, w["ffn_w_down"], w["final_norm_w"], seq_tiles, grid_conv, mod_of_tile)
    return y, s_h, s_r


def _place_rows(w, start, total):
    return jnp.zeros((total, w.shape[1]), w.dtype).at[start:start + w.shape[0]].set(w)


def kernel(x_prompt, x_sample, state_hgrn, state_rwkv, c, c_ctx, ada_w, ada_b, norm_mix_w, w_in, hgrn_lb, hgrn_norm_w, rwkv_conv, rwkv_w0, rwkv_w2, rwkv_a0, rwkv_a2, rwkv_g2, rwkv_k_k, rwkv_k_a, rwkv_r_k, rwkv_ln_w, rwkv_ln_b, w_out, norm_ffn_w, ffn_w_gate, ffn_w_up, ffn_conv, ffn_conv_b, ffn_w_down, final_norm_w):
    assert w_in.shape[0] == 1, "one trunk layer"
    b_ctx, t_ctx, _ = x_prompt.shape
    b_lat, t_lat, _ = x_sample.shape
    row = lambda v: v.reshape(1, -1)
    pad_cols = P_B_PAD - P_B
    w_dec = jnp.concatenate([_place_rows(rwkv_w2[0, 0], 0, LORA_COLS), _place_rows(rwkv_w2[0, 1], LORA_W, LORA_COLS)], axis=1)
    pairs = N_HEADS_B // 2
    w_dec = w_dec.reshape(LORA_COLS, 2, pairs, LANES).transpose(2, 0, 1, 3).reshape(pairs, LORA_COLS, 2 * LANES)
    w_dec = jnp.stack(_split(w_dec), axis=1)
    weights = dict(
        norm_mix_w=row(norm_mix_w[0]),
        w_in_a=w_in[0, :, :P_A].astype(BF16),
        w_in_b=jnp.pad(w_in[0, :, P_A:], ((0, 0), (0, pad_cols))).astype(BF16),
        rwkv_conv=jnp.pad(rwkv_conv[0], ((0, 0), (0, pad_cols))),
        hgrn_lb=hgrn_lb,
        hgrn_norm_w=row(hgrn_norm_w[0]),
        rwkv=dict(w0=rwkv_w0[0], a0=row(rwkv_a0[0]), k_k=row(rwkv_k_k[0]), k_a=row(rwkv_k_a[0]), r_k=row(rwkv_r_k[0]),
                  ln_w=row(rwkv_ln_w[0]), ln_b=row(rwkv_ln_b[0]), w_dec=w_dec,
                  w_a=_place_rows(rwkv_a2[0], 2 * LORA_W, LORA_COLS),
                  w_g=_place_rows(rwkv_g2[0], 2 * LORA_W + LORA_A, LORA_COLS)),
        w_out_a=w_out[0, :WIDTH_A].astype(BF16),
        w_out_b=w_out[0, WIDTH_A:].astype(BF16),
        norm_ffn_w=row(norm_ffn_w[0]),
        ffn_w_gate=ffn_w_gate[0].astype(BF16),
        ffn_w_up=ffn_w_up[0].astype(BF16),
        ffn_conv=ffn_conv[0].reshape(9, D_FF),
        ffn_conv_b=row(ffn_conv_b[0]),
        ffn_w_down=ffn_w_down[0].astype(BF16),
        final_norm_w=row(final_norm_w),
    )
    cvec = jnp.concatenate([c_ctx[None, :], c, jnp.zeros((8 - 1 - b_lat, D_MODEL), F32)], axis=0)
    mod = _modulation(cvec, ada_w, ada_b).reshape(8, 1, 6 * D_MODEL)

    yp, s_h, s_r = _block(x_prompt.reshape(b_ctx * t_ctx, D_MODEL), t_ctx, mod, lambda i: 0, None, None, False, weights)
    lat_tiles = t_lat // ROW_TILE
    ys, _, _ = _block(x_sample.reshape(b_lat * t_lat, D_MODEL), t_lat, mod, lambda i: 1 + i // lat_tiles,
                      state_hgrn[:, 0], state_rwkv[:, 0], True, weights)
    y_prompt = yp.reshape(b_ctx, t_ctx, D_MODEL)
    y_sample = ys.reshape(b_lat, t_lat, D_MODEL)
    new_state_hgrn = s_h[:, None]
    new_state_rwkv = s_r[:, None]
    return (y_prompt, y_sample, new_state_hgrn, new_state_rwkv)
```

```python
import functools

import numpy as np
import jax
import jax.numpy as jnp
from jax import lax
from jax.experimental import pallas as pl
from jax.experimental.pallas import tpu as pltpu

F32 = jnp.float32
BF16 = jnp.bfloat16

D_MODEL = 1024
GRID_W = 64
WIDTH_A = 512
HEAD_A = 128
N_HEADS_A = 4
WIDTH_B = 512
HEAD_B = 64
N_HEADS_B = 8
LORA_W = 32
LORA_A = 32
LORA_G = 96
D_FF = 2816
RMS_EPS = 1e-6
GN_EPS = 64e-5
DECAY_SCALE = 0.6065306597
P_A = 5 * WIDTH_A
P_B = 3 * WIDTH_B + 2 * LORA_W + LORA_A + LORA_G
LORA_COLS = 256
P_B_PAD = 3 * WIDTH_B + LORA_COLS

CHUNK = 64
LANES = 128
SUBLANES = 8
ROW_TILE = 256
MIX_OUT_ROWS = 256
FFN_OUT_ROWS = 512
FFN_COL_BLOCK = 256
VMEM_LIMIT = 56 * 1024 * 1024

NN = (((1,), (0,)), ((), ()))
NT = (((1,), (1,)), ((), ()))
TN = (((0,), (0,)), ((), ()))


def _dot(a, b, dims=NN):
    return lax.dot_general(a, b, dims, preferred_element_type=F32)


def _split(a):
    hi = a.astype(BF16)
    lo = (a - hi.astype(F32)).astype(BF16)
    return hi, lo


def _mm(a, b, dims=NN, passes=1):
    if passes == 1:
        return _dot(a.astype(BF16), b.astype(BF16), dims)
    ah, al = _split(a)
    bh, bl = _split(b)
    return _dot(ah, bh, dims) + (_dot(ah, bl, dims) + _dot(al, bh, dims))


def _mm_exact_lhs(m, x):
    n = x.shape[1]
    both = _dot(m, jnp.concatenate(_split(x), axis=1))
    return both[:, :n] + both[:, n:]


def _sigmoid(x):
    return 0.5 * jnp.tanh(0.5 * x) + 0.5


def _silu(x):
    return x * _sigmoid(x)


def _hgrn_level_consts(c, rev):
    t = np.arange(c)
    mats = [(t[None, :] <= t[:, None]).astype(np.float32)]
    masks = [np.eye(c, dtype=np.float32)]
    n = 1
    while n < c:
        blk, half = t // (2 * n), (t // n) % 2
        mid = blk * 2 * n + n
        m = np.zeros((c, c), np.float32)
        for row in range(c):
            if half[row] == 1:
                m[row, mid[row]:row + 1] = 1.0
            else:
                m[row, row + 1:mid[row]] = 1.0
        mats.append(m)
        masks.append(((blk[:, None] == blk[None, :]) & (half[:, None] == 1) & (half[None, :] == 0))
                     .astype(np.float32))
        n *= 2
    if rev:
        mats = [m[::-1, ::-1] for m in mats]
        masks = [m[::-1, ::-1] for m in masks]
    return np.concatenate(mats, 0), np.concatenate(masks, 0)


def _hgrn_consts(c):
    mf, kf = _hgrn_level_consts(c, False)
    mb, kb = _hgrn_level_consts(c, True)
    mat_levels = 1 + int(np.log2(SUBLANES))
    return np.stack([mf, mb])[:, :mat_levels * c], np.stack([kf, kb])


def _rwkv_consts(c):
    t = np.arange(c)
    out = []
    for rev in (False, True):
        incl = (t[None, :] <= t[:, None]) if not rev else (t[None, :] >= t[:, None])
        strict = (t[None, :] < t[:, None]) if not rev else (t[None, :] > t[:, None])
        z = np.zeros((c, c), bool)
        cum = np.block([[incl, z], [z, z]])
        out.append(np.stack([cum, np.block([[strict, z], [z, strict]]), np.block([[incl, z], [z, incl]])]))
    return np.stack(out).astype(np.float32)


def _each(fn, *cols):
    return [fn(*args) for args in zip(*cols)]


def _interleave(*gens):
    results = [None] * len(gens)
    live = dict(enumerate(gens))
    while live:
        for i in list(live):
            try:
                next(live[i])
            except StopIteration as stop:
                results[i] = stop.value
                del live[i]
    return results


def _pair_exponent(b, n, rev):
    parts = []
    for base in range(0, b.shape[0], 2 * n):
        first, second = b[base:base + n], b[base + n:base + 2 * n]
        if rev:
            r = jnp.broadcast_to(b[base + n:base + n + 1], first.shape)
            parts += [first - r, r - second]
        else:
            r = jnp.broadcast_to(b[base + n - 1:base + n], first.shape)
            parts += [r - first, second - r]
    return jnp.concatenate(parts, axis=0)


def _hgrn_chunks_local(units):
    q, k, v, g, m, masks, rev = (list(col) for col in zip(*units))
    c = q[0].shape[0]
    levels = masks[0].shape[0] // c
    mat_levels = m[0].shape[0] // c
    e_all = _each(_mm_exact_lhs, m, g)
    yield
    b = [x[:c] for x in e_all]
    edge = _each(lambda x, rv: x[0:1] if rv else x[c - 1:c], b, rev)
    kv = _each(lambda vv, kk, ed, bb: _mm(vv, kk * jnp.exp(ed - bb), TN), v, k, edge, b)
    yield
    sc = _each(lambda mk, a, kk: mk[:c] * _mm(a, kk, NT), masks, q, k)
    yield
    for l in range(1, levels):
        if l < mat_levels:
            e = _each(lambda x: jnp.exp(x[l * c:(l + 1) * c]), e_all)
        else:
            e = _each(lambda bb, rv: jnp.exp(_pair_exponent(bb, 1 << (l - 1), rv)), b, rev)
        sc = _each(lambda s, mk, a, kk, ee: s + mk[l * c:(l + 1) * c] * _mm(a * ee, kk * ee, NT), sc, masks, q, k, e)
        yield
    intra = _each(_mm, sc, v)
    yield
    return [(ii, qq * jnp.exp(bb), kvv, jnp.exp(ed)) for ii, qq, bb, kvv, ed in zip(intra, q, b, kv, edge)]


def _hgrn_chunk_state(st, part):
    intra, qb, kv, decay = part
    return intra + _mm(qb, st, NT), st * decay + kv


def _stack_heads(x, lane_head):
    return jnp.concatenate([jnp.where(lane_head == 0, x, 0.0), jnp.where(lane_head == 1, x, 0.0)], axis=0)


RWKV_PASSES_SCORE = 1
RWKV_PASSES_SOLVE = 1
RWKV_PASSES_STATE = 1
RWKV_LOCAL_PARTS = 7
RWKV_UNITS = 8
HGRN_UNITS = 16
MIXER_ROW_BLOCK = 256


def _rwkv_chunks_local(units):
    c = units[0][0].shape[0]
    lane_head = lax.broadcasted_iota(jnp.int32, (c, LANES), 1) // HEAD_B
    st = lambda x: _stack_heads(x, lane_head)
    r, k, v, kk, beta, lw, consts, rev = (list(col) for col in zip(*units))
    strict = [cs[1] for cs in consts]
    incl = [cs[2] for cs in consts]
    g = _each(lambda cs, x: _mm_exact_lhs(cs[0][:c, :c].astype(BF16), x), consts, lw)
    edge = _each(lambda x, rv: x[0:1] if rv else x[c - 1:c], g, rev)
    yield
    kg = _each(lambda x, gg, l: st(x * jnp.exp(gg - l)), kk, g, lw)
    rg = _each(lambda x, gg: st(x * jnp.exp(gg)), r, g)
    e_out = _each(lambda gg: jnp.exp(-gg), g)
    bi = _each(lambda x, e: st(x * e), beta, e_out)
    ki = _each(lambda x, e: st(x * e), k, e_out)
    vs = _each(st, v)
    a_all = _each(lambda a, b, cc, d: _mm(jnp.concatenate([a, b], 0), jnp.concatenate([cc, d], 0), NT, RWKV_PASSES_SCORE),
                  kg, rg, bi, ki)
    yield
    n = _each(lambda m, a: m * a[:2 * c, :2 * c], strict, a_all)
    a_ak = _each(lambda m, a: m * a[:2 * c, 2 * c:], strict, a_all)
    a_rb = _each(lambda m, a: m * a[2 * c:, :2 * c], incl, a_all)
    a_rk = _each(lambda m, a: m * a[2 * c:, 2 * c:], incl, a_all)
    av = _each(lambda a, b, x: _mm(jnp.concatenate([a, b], 0), x, passes=RWKV_PASSES_SCORE), a_ak, a_rk, vs)
    yield
    eye = (lax.broadcasted_iota(jnp.int32, (2 * c, 2 * c), 0) == lax.broadcasted_iota(jnp.int32, (2 * c, 2 * c), 1)).astype(F32)
    tinv = _each(lambda a: eye - a, n)
    power = _each(lambda a: _mm(a, a, passes=RWKV_PASSES_SOLVE), n)
    yield
    span = 4
    while span < c:
        both = _each(lambda p, t: _mm(jnp.concatenate([p, t], axis=0), p, passes=RWKV_PASSES_SOLVE), power, tinv)
        power = [b[:2 * c] for b in both]
        tinv = _each(lambda t, b: t + b[2 * c:], tinv, both)
        span *= 2
        yield
    tinv = _each(lambda p, t: t + _mm(t, p, passes=RWKV_PASSES_SOLVE), power, tinv)
    yield
    x = _each(lambda t, a, b: _mm(t, jnp.concatenate([a, b[:2 * c]], axis=1), passes=RWKV_PASSES_SOLVE), tinv, kg, av)
    yield
    e_edge = _each(lambda ed, gg: jnp.exp(ed - gg), edge, g)
    kv = _each(lambda a, b, e: _mm(a, st(b * e), TN, RWKV_PASSES_STATE), vs, k, e_edge)
    yield
    bgc = _each(lambda b, e: st(b * e), beta, e_edge)
    return [((xx[:, :LANES], rr, xx[:, LANES:], ab, a[2 * c:], bg, kvv), jnp.exp(ed))
            for xx, rr, ab, a, bg, kvv, ed in zip(x, rg, a_rb, av, bgc, kv, edge)]


def _rwkv_chunks_state(states, parts, decays):
    w1, rg, x2, a_rb, y0, bgc, kv = (list(col) for col in zip(*parts))
    c2 = w1[0].shape[0]
    hs = _each(lambda a, b, s: _mm(jnp.concatenate([a, b], 0), s, NT, RWKV_PASSES_STATE), w1, rg, states)
    yield
    u = _each(lambda h, x: -(h[:c2] + x), hs, x2)
    s = _each(lambda s_, d, uu, b, kv_: s_ * d + _mm(uu, b, TN, RWKV_PASSES_STATE) + kv_, states, decays, u, bgc, kv)
    yield
    y = _each(lambda h, y_, a, uu: h[c2:] + y_ + _mm(a, uu, passes=RWKV_PASSES_STATE), hs, y0, a_rb, u)
    yield
    return [yy[:c2 // 2] + yy[c2 // 2:] for yy in y], s


def _params(*semantics):
    return pltpu.CompilerParams(dimension_semantics=semantics, vmem_limit_bytes=VMEM_LIMIT)


def _full(shape):
    return pl.BlockSpec(shape, lambda *_: (0,) * len(shape))


def _modulated_norm(x, norm_w, scale, shift):
    y = x * lax.rsqrt(jnp.mean(x * x, axis=-1, keepdims=True) + RMS_EPS)
    return (y * norm_w) * (1.0 + scale) + shift


def _mod_kernel(c_ref, w_ref, b_ref, o_ref):
    o_ref[...] = _mm(_silu(c_ref[...]), w_ref[...], passes=3) + b_ref[...]


def _modulation(cvec, ada_w, ada_b):
    n = ada_w.shape[2]
    tn = n // 4
    return pl.pallas_call(
        _mod_kernel,
        grid=(n // tn,),
        in_specs=[_full(cvec.shape), pl.BlockSpec((None, D_MODEL, tn), lambda j: (0, 0, j)),
                  pl.BlockSpec((1, tn), lambda j: (0, j))],
        out_specs=pl.BlockSpec((cvec.shape[0], tn), lambda j: (0, j)),
        out_shape=jax.ShapeDtypeStruct((cvec.shape[0], n), F32),
        compiler_params=_params("arbitrary"),
        name="modulation",
    )(cvec, ada_w, ada_b)


def _in_proj_kernel(x_ref, xp_ref, xn_ref, mod_ref, nw_ref, wa_ref, wb_ref, cw_ref, ua_ref, ub_ref, *, seq_tiles):
    i = pl.program_id(0)
    mod = mod_ref[...]
    shift, scale = mod[:, 0:D_MODEL], mod[:, D_MODEL:2 * D_MODEL]
    nw = nw_ref[...]
    h = _modulated_norm(x_ref[...], nw, scale, shift).astype(BF16)
    ua_ref[...] = _dot(h, wa_ref[...])
    halo = jnp.concatenate([xp_ref[...], xn_ref[...]], axis=0)
    hh = _modulated_norm(halo, nw, scale, shift).astype(BF16)
    ub_all = _dot(jnp.concatenate([h, hh], axis=0), wb_ref[...])
    n = h.shape[0]
    ub, ubh = ub_all[:n], ub_all[n:]
    first = (i % seq_tiles) == 0
    last = (i % seq_tiles) == seq_tiles - 1
    prev_row = jnp.where(first, 0.0, ubh[7:8])
    next_row = jnp.where(last, 0.0, ubh[8:9])
    rows = lax.broadcasted_iota(jnp.int32, ub.shape, 0)
    below = jnp.where(rows == 0, prev_row, pltpu.roll(ub, 1, 0))
    above = jnp.where(rows == n - 1, next_row, pltpu.roll(ub, n - 1, 0))
    cw = cw_ref[...]
    ub_ref[...] = cw[0:1] * below + cw[1:2] * ub + cw[2:3] * above


def _in_proj(x, mod, norm_w, w_a, w_b, conv_w, seq_tiles, mod_of_tile):
    t = x.shape[0]
    nt = t // ROW_TILE
    sub = ROW_TILE // 8
    return pl.pallas_call(
        functools.partial(_in_proj_kernel, seq_tiles=seq_tiles),
        grid=(nt,),
        in_specs=[
            pl.BlockSpec((ROW_TILE, D_MODEL), lambda i: (i, 0)),
            pl.BlockSpec((8, D_MODEL), lambda i: (jnp.maximum(i * sub - 1, 0), 0)),
            pl.BlockSpec((8, D_MODEL), lambda i: (jnp.minimum((i + 1) * sub, nt * sub - 1), 0)),
            pl.BlockSpec((None, 1, 6 * D_MODEL), lambda i: (mod_of_tile(i), 0, 0)),
            _full((1, D_MODEL)), _full(w_a.shape), _full(w_b.shape), _full(conv_w.shape),
        ],
        out_specs=[pl.BlockSpec((ROW_TILE, P_A), lambda i: (i, 0)), pl.BlockSpec((ROW_TILE, P_B_PAD), lambda i: (i, 0))],
        out_shape=[jax.ShapeDtypeStruct((t, P_A), F32), jax.ShapeDtypeStruct((t, P_B_PAD), F32)],
        compiler_params=_params("parallel"),
        name="in_proj",
    )(x, x, x, mod, norm_w, w_a, w_b, conv_w)


def _hgrn_kernel(*refs, ns, nc, group, block, zero_state):
    q_ref, i_ref, zf_ref, zb_ref, g_ref, lb_ref, nw_ref = refs[:7]
    s0_ref = None if zero_state else refs[7]
    m_ref, mask_ref, o_ref, sout_ref, acc_s, st_ref = refs[-6:]
    c = CHUNK
    seq_len = nc * c
    chains = [(s, d) for s in range(ns) for d in range(2)]
    p0, p1 = lb_ref[0], lb_ref[1]
    mx = jnp.maximum(p0, p1)
    e0, e1 = jnp.exp(p0 - mx), jnp.exp(p1 - mx)
    lb = e0 / (e0 + e1)
    for ch, (s, d) in enumerate(chains):
        st_ref[ch] = jnp.zeros((HEAD_A, HEAD_A), F32) if zero_state else s0_ref[s, d].T

    def rows_of(gi, j, s, d):
        cf = gi * group + j
        ci = cf if d == 0 else nc - 1 - cf
        return pl.ds(pl.multiple_of(s * seq_len + ci * c, c), c)

    def scan_group(gi, carry):
        units = []
        for j in range(group):
            for s, d in chains:
                rows = rows_of(gi, j, s, d)
                lo = lb[d:d + 1]
                f = lo + (1.0 - lo) * _sigmoid((zf_ref, zb_ref)[d][rows, :])
                units.append((_silu(q_ref[rows, :]), 1.0 - f, i_ref[rows, :], jnp.log(f), m_ref[d], mask_ref[d], d == 1))
        parts, = _interleave(_hgrn_chunks_local(units))
        st = [st_ref[ch] for ch in range(len(chains))]
        for j in range(group):
            for ch, (s, d) in enumerate(chains):
                o, st[ch] = _hgrn_chunk_state(st[ch], parts[j * len(chains) + ch])
                acc_s[rows_of(gi, j, s, d), :] += o
        for ch, state in enumerate(st):
            st_ref[ch] = state
        return carry

    acc_s[...] = jnp.zeros(acc_s.shape, F32)
    lax.fori_loop(0, nc // group, scan_group, 0)

    def finish(bi, carry):
        rows = pl.ds(pl.multiple_of(bi * block, block), block)
        o = acc_s[rows, :]
        o = o * lax.rsqrt(jnp.mean(o * o, axis=-1, keepdims=True) + RMS_EPS) * nw_ref[...]
        o_ref[rows, :] = o * _silu(g_ref[rows, :])
        return carry

    lax.fori_loop(0, ns * seq_len // block, finish, 0, unroll=2 if ns * seq_len // block % 2 == 0 else 1)
    for ch, (s, d) in enumerate(chains):
        sout_ref[s, d] = st_ref[ch].T


def _hgrn_mixer(u_a, lb_raw, norm_w, s0, seq_len):
    t = u_a.shape[0]
    nb = t // seq_len
    nc = seq_len // CHUNK
    group = min(HGRN_UNITS // 2, nc)
    ns = max(1, min(HGRN_UNITS // (2 * group), nb))
    assert nc % group == 0 and nb % ns == 0
    rows = ns * seq_len
    mats, masks = _hgrn_consts(CHUNK)
    mats = jnp.asarray(mats, BF16)
    masks = jnp.asarray(masks, F32)
    col = lambda part: pl.BlockSpec((rows, HEAD_A), lambda b, h: (b, part * N_HEADS_A + h))
    state_spec = pl.BlockSpec((ns, None, 2, None, HEAD_A, HEAD_A), lambda b, h: (b, 0, 0, h, 0, 0))
    state_in = ([], []) if s0 is None else ([state_spec], [s0])
    return pl.pallas_call(
        functools.partial(_hgrn_kernel, ns=ns, nc=nc, group=group, block=min(MIXER_ROW_BLOCK, rows),
                          zero_state=s0 is None),
        grid=(nb // ns, N_HEADS_A),
        in_specs=[col(0), col(1), col(2), col(3), col(4),
                  pl.BlockSpec((2, 2, HEAD_A), lambda b, h: (0, 0, h)),
                  pl.BlockSpec((1, HEAD_A), lambda b, h: (0, h)),
                  *state_in[0], _full(mats.shape), _full(masks.shape)],
        out_specs=[pl.BlockSpec((rows, HEAD_A), lambda b, h: (b, h)), state_spec],
        out_shape=[jax.ShapeDtypeStruct((t, WIDTH_A), F32),
                   jax.ShapeDtypeStruct((nb, 1, 2, N_HEADS_A, HEAD_A, HEAD_A), F32)],
        scratch_shapes=[pltpu.VMEM((rows, HEAD_A), F32), pltpu.VMEM((2 * ns, HEAD_A, HEAD_A), F32)],
        compiler_params=_params("parallel", "parallel"),
        name="hgrn2_scan",
    )(u_a, u_a, u_a, u_a, u_a, lb_raw, norm_w, *state_in[1], mats, masks)


def _mm_exact_rhs(x, m):
    n = x.shape[0]
    both = _dot(jnp.concatenate(_split(x), axis=0), m)
    return both[:n] + both[n:]


def _rwkv_kernel(*refs, ns, nc, group, block, zero_state):
    (r_ref, k_ref, v_ref, lora_ref, w0_ref, a0_ref, kkw_ref, ka_ref, rk_ref, lnw_ref, lnb_ref,
     wdec_ref, wa_ref, wg_ref) = refs[:14]
    s0_ref = None if zero_state else refs[14]
    (consts_ref, seg_ref, o_ref, sout_ref, lwf_s, lwb_s, kk_s, beta_s, kmod_s, y_s,
     loca_s, deca_s, locb_s, decb_s, st_ref) = refs[-15:]
    c = CHUNK
    seq_len = nc * c
    chains = [(s, d) for s in range(ns) for d in range(2)]
    seg = seg_ref[...]

    for ch, (s, d) in enumerate(chains):
        st_ref[ch] = jnp.zeros((LANES, LANES), F32)
        if not zero_state:
            st_ref[ch, 0:HEAD_B, 0:HEAD_B] = s0_ref[s, d, 0]
            st_ref[ch, HEAD_B:LANES, HEAD_B:LANES] = s0_ref[s, d, 1]

    def block_of(bi):
        return pl.ds(bi * block, block)

    edge_chunks = list(range(group)) + list(range(nc - group, nc))
    edge_blocks = sorted({(s * seq_len + ci * c) // block for s in range(ns) for ci in edge_chunks})
    inner_blocks = [bi for bi in range(ns * seq_len // block) if bi not in edge_blocks]

    def one_per_stage(fn, blocks):
        for bi in blocks:
            fn(bi)
            yield

    def prepare(bi):
        rows = block_of(bi)
        lora = lora_ref[rows, :]
        th, tl = _split(jnp.tanh(lora))
        w_hi, w_lo = wdec_ref[0], wdec_ref[1]
        dec = _dot(th, w_hi) + (_dot(th, w_lo) + _dot(tl, w_hi))
        w0 = w0_ref[...]
        lwf_s[rows, :] = -DECAY_SCALE * _sigmoid(w0[0:1] + dec[:, :LANES])
        lwb_s[rows, :] = -DECAY_SCALE * _sigmoid(w0[1:2] + dec[:, LANES:])
        a = _sigmoid(a0_ref[...] + _mm(lora, wa_ref[...]))
        k = k_ref[rows, :]
        kk = k * kkw_ref[...]
        kk = kk * lax.rsqrt(_mm(kk * kk, seg) + 1e-12)
        kk_s[rows, :] = kk
        beta_s[rows, :] = kk * a
        kmod_s[rows, :] = k * (1.0 + (a - 1.0) * ka_ref[...])

    for bi in edge_blocks:
        prepare(bi)

    def rows_of(gi, j, s, d):
        cf = gi * group + j
        ci = cf if d == 0 else nc - 1 - cf
        return pl.ds(pl.multiple_of(s * seq_len + ci * c, c), c)

    def local_stage(gi, loc, dec):
        units = []
        for j in range(group):
            for s, d in chains:
                rows = rows_of(gi, j, s, d)
                units.append((r_ref[rows, :], kmod_s[rows, :], v_ref[rows, :], kk_s[rows, :], beta_s[rows, :],
                              (lwf_s, lwb_s)[d][rows, :], consts_ref[d], d == 1))
        results = yield from _rwkv_chunks_local(units)
        for u, (parts, decay) in enumerate(results):
            j, ch = divmod(u, len(chains))
            for idx, part in enumerate(parts):
                loc[ch, j, idx] = part
            dec[ch, j] = decay

    def state_stage(gi, loc, dec):
        states = [st_ref[ch] for ch in range(len(chains))]
        for j in range(group):
            parts = [[loc[ch, j, idx] for idx in range(RWKV_LOCAL_PARTS)] for ch in range(len(chains))]
            ys, states = yield from _rwkv_chunks_state(states, parts, [dec[ch, j] for ch in range(len(chains))])
            for y, (s, d) in zip(ys, chains):
                y_s[rows_of(gi, j, s, d), :] += y
        for ch, state in enumerate(states):
            st_ref[ch] = state

    y_s[...] = jnp.zeros(y_s.shape, F32)
    ng = nc // group
    _interleave(local_stage(0, loca_s, deca_s), one_per_stage(prepare, inner_blocks))

    def two_groups(p, carry):
        _interleave(local_stage(2 * p + 1, locb_s, decb_s), state_stage(2 * p, loca_s, deca_s))
        _interleave(local_stage(2 * p + 2, loca_s, deca_s), state_stage(2 * p + 1, locb_s, decb_s))
        return carry

    lax.fori_loop(0, ng // 2 - 1, two_groups, 0)
    _interleave(local_stage(ng - 1, locb_s, decb_s), state_stage(ng - 2, loca_s, deca_s))
    def finish(bi):
        rows = block_of(bi)
        y = y_s[rows, :]
        inv_n = 1.0 / HEAD_B
        mu = _mm_exact_rhs(y, seg) * inv_n
        dy = y - mu
        var = _mm(dy * dy, seg) * inv_n
        yn = dy * lax.rsqrt(var + GN_EPS) * lnw_ref[...] + lnb_ref[...]
        v = v_ref[rows, :]
        bonus = _mm(r_ref[rows, :] * kmod_s[rows, :] * rk_ref[...], seg) * v
        g = _mm(_sigmoid(lora_ref[rows, :]), wg_ref[...])
        o_ref[rows, :] = (yn + bonus) * g

    _interleave(state_stage(ng - 1, locb_s, decb_s), one_per_stage(finish, inner_blocks))
    for bi in edge_blocks:
        finish(bi)
    for ch, (s, d) in enumerate(chains):
        sout_ref[s, d, 0] = st_ref[ch, 0:HEAD_B, 0:HEAD_B]
        sout_ref[s, d, 1] = st_ref[ch, HEAD_B:LANES, HEAD_B:LANES]


def _rwkv_mixer(u_b, p, s0, seq_len):
    t = u_b.shape[0]
    nb = t // seq_len
    nc = seq_len // CHUNK
    group = max(1, min(RWKV_UNITS // 2, nc // 4))
    ns = max(1, min(RWKV_UNITS // (2 * group), nb))
    assert nc % (2 * group) == 0 and nb % ns == 0
    rows = ns * seq_len
    block = min(MIXER_ROW_BLOCK, rows)
    pairs = N_HEADS_B // 2
    consts = jnp.asarray(_rwkv_consts(CHUNK), F32)
    lane = np.arange(LANES) // HEAD_B
    seg = jnp.asarray(lane[:, None] == lane[None, :], BF16)
    col = lambda part: pl.BlockSpec((rows, LANES), lambda b, h: (b, part * pairs + h))
    vec = lambda n: pl.BlockSpec((n, LANES), lambda b, h: (0, h))
    state_spec = pl.BlockSpec((ns, None, 2, 2, HEAD_B, HEAD_B), lambda b, h: (b, 0, 0, h, 0, 0))
    scr = lambda: pltpu.VMEM((rows, LANES), F32)
    loc = lambda: pltpu.VMEM((2 * ns, group, RWKV_LOCAL_PARTS, LANES, LANES), F32)
    dec = lambda: pltpu.VMEM((2 * ns, group, 1, LANES), F32)
    state_in = ([], []) if s0 is None else ([state_spec], [s0])
    return pl.pallas_call(
        functools.partial(_rwkv_kernel, ns=ns, nc=nc, group=group, block=block, zero_state=s0 is None),
        grid=(nb // ns, pairs),
        in_specs=[col(0), col(1), col(2),
                  pl.BlockSpec((rows, LORA_COLS), lambda b, h: (b, 3 * WIDTH_B // LORA_COLS)),
                  vec(2), vec(1), vec(1), vec(1), vec(1), vec(1), vec(1),
                  pl.BlockSpec((None, 2, LORA_COLS, 2 * LANES), lambda b, h: (h, 0, 0, 0)),
                  pl.BlockSpec((LORA_COLS, LANES), lambda b, h: (0, h)),
                  pl.BlockSpec((LORA_COLS, LANES), lambda b, h: (0, h)),
                  *state_in[0], _full(consts.shape), _full(seg.shape)],
        out_specs=[pl.BlockSpec((rows, LANES), lambda b, h: (b, h)), state_spec],
        out_shape=[jax.ShapeDtypeStruct((t, WIDTH_B), F32),
                   jax.ShapeDtypeStruct((nb, 1, 2, N_HEADS_B, HEAD_B, HEAD_B), F32)],
        scratch_shapes=[scr(), scr(), scr(), scr(), scr(), scr(), loc(), dec(), loc(), dec(),
                        pltpu.VMEM((2 * ns, LANES, LANES), F32)],
        compiler_params=_params("parallel", "parallel"),
        name="rwkv7_scan",
    )(u_b, u_b, u_b, u_b, p["w0"], p["a0"], p["k_k"], p["k_a"], p["r_k"], p["ln_w"], p["ln_b"],
      p["w_dec"], p["w_a"], p["w_g"], *state_in[1], consts, seg)


def _mix_out_kernel(oa_ref, ob_ref, x_ref, mod_ref, woa_ref, wob_ref, nw_ref, wg_ref, wu_ref, cw_ref, cb_ref,
                    x1_ref, c_ref, u_ref, *, grid_conv, width):
    mod = mod_ref[...]
    gate1 = mod[:, 2 * D_MODEL:3 * D_MODEL]
    shift2, scale2 = mod[:, 3 * D_MODEL:4 * D_MODEL], mod[:, 4 * D_MODEL:5 * D_MODEL]
    mix = _dot(oa_ref[...].astype(BF16), woa_ref[...]) + _dot(ob_ref[...].astype(BF16), wob_ref[...])
    x1 = x_ref[...] + gate1 * mix
    x1_ref[...] = x1
    h = _modulated_norm(x1, nw_ref[...], scale2, shift2).astype(BF16)
    n = h.shape[0]
    col = lax.broadcasted_iota(jnp.int32, (width, FFN_COL_BLOCK), 0)
    blocks = [slice(j * FFN_COL_BLOCK, (j + 1) * FFN_COL_BLOCK) for j in range(D_FF // FFN_COL_BLOCK)]
    g_next = _dot(h, wg_ref[:, blocks[0]])
    for j, cols in enumerate(blocks):
        g_all = g_next
        u_ref[:, cols] = _dot(h, wu_ref[:, cols]).astype(BF16)
        if j + 1 < len(blocks):
            g_next = _dot(h, wg_ref[:, blocks[j + 1]])
        cw = cw_ref[:, cols].astype(BF16)
        bias = cb_ref[:, cols].astype(BF16)
        for r0 in range(0, n, width):
            g = g_all[r0:r0 + width]
            left = jnp.where(col == 0, 0.0, pltpu.roll(g, 1, 0)).astype(BF16)
            right = jnp.where(col == width - 1, 0.0, pltpu.roll(g, width - 1, 0)).astype(BF16)
            g = g.astype(BF16)
            for slot, dr in enumerate((0, 1, 2) if grid_conv else (1,)):
                taps = cw[3 * dr:3 * dr + 1] * left + cw[3 * dr + 1:3 * dr + 2] * g + cw[3 * dr + 2:3 * dr + 3] * right
                c_ref[slot, r0:r0 + width, cols] = taps + bias if dr == 1 else taps


def _mix_out(o_a, o_b, x, mod, w_out_a, w_out_b, norm_w, w_gate, w_up, conv_w, conv_b, seq_len, grid_conv, mod_of_tile):
    t = x.shape[0]
    nr = 3 if grid_conv else 1
    tile = MIX_OUT_ROWS
    scale = tile // ROW_TILE
    width = GRID_W if grid_conv else seq_len
    assert tile % width == 0
    row = lambda n: pl.BlockSpec((tile, n), lambda i: (i, 0))
    return pl.pallas_call(
        functools.partial(_mix_out_kernel, grid_conv=grid_conv, width=width),
        grid=(t // tile,),
        in_specs=[row(WIDTH_A), row(WIDTH_B), row(D_MODEL),
                  pl.BlockSpec((None, 1, 6 * D_MODEL), lambda i: (mod_of_tile(i * scale), 0, 0)),
                  _full(w_out_a.shape), _full(w_out_b.shape), _full((1, D_MODEL)), _full(w_gate.shape), _full(w_up.shape),
                  _full(conv_w.shape), _full(conv_b.shape)],
        out_specs=[row(D_MODEL), pl.BlockSpec((nr, tile, D_FF), lambda i: (0, i, 0)), row(D_FF)],
        out_shape=[jax.ShapeDtypeStruct((t, D_MODEL), F32), jax.ShapeDtypeStruct((nr, t, D_FF), BF16),
                   jax.ShapeDtypeStruct((t, D_FF), BF16)],
        compiler_params=_params("parallel"),
        name="mix_out_ffn_in",
    )(o_a, o_b, x, mod, w_out_a, w_out_b, norm_w, w_gate, w_up, conv_w, conv_b)


def _gelu_tanh(x):
    half = 0.5 * x
    return half + half * jnp.tanh(x * (0.7978845608028654 + (0.7978845608028654 * 0.044715) * (x * x)))


def _ffn_out_kernel(*refs, seq_tiles, grid_conv):
    if grid_conv:
        c_ref, ca_ref, cb_ref, u_ref, x1_ref, mod_ref, wd_ref, fw_ref, y_ref = refs
        i = pl.program_id(0)
        top = (i % seq_tiles) == 0
        bottom = (i % seq_tiles) == seq_tiles - 1
        n = c_ref.shape[1]
        zero = jnp.zeros((), BF16)
        up_rows = jnp.concatenate([jnp.where(top, zero, ca_ref[...]), c_ref[0, :n - GRID_W, :]], axis=0)
        down_rows = jnp.concatenate([c_ref[2, GRID_W:, :], jnp.where(bottom, zero, cb_ref[...])], axis=0)
        gt = (c_ref[1] + up_rows + down_rows).astype(F32)
    else:
        c_ref, u_ref, x1_ref, mod_ref, wd_ref, fw_ref, y_ref = refs
        gt = c_ref[0].astype(F32)
    act = (_gelu_tanh(gt) * u_ref[...].astype(F32)).astype(BF16)
    mod = mod_ref[...]
    gate2 = mod[:, 5 * D_MODEL:6 * D_MODEL]
    x2 = x1_ref[...] + gate2 * _dot(act, wd_ref[...])
    y_ref[...] = x2 * lax.rsqrt(jnp.mean(x2 * x2, axis=-1, keepdims=True) + RMS_EPS) * fw_ref[...]


def _ffn_out(c, up, x1, mod, w_down, final_w, seq_tiles, grid_conv, mod_of_tile):
    t = x1.shape[0]
    tile = FFN_OUT_ROWS
    scale = tile // ROW_TILE
    nt = t // tile
    per = tile // GRID_W
    row = lambda n: pl.BlockSpec((tile, n), lambda i: (i, 0))
    c_specs = [pl.BlockSpec((c.shape[0], tile, D_FF), lambda i: (0, i, 0))]
    c_args = [c]
    if grid_conv:
        c_specs += [pl.BlockSpec((None, GRID_W, D_FF), lambda i: (0, jnp.maximum(i * per - 1, 0), 0)),
                    pl.BlockSpec((None, GRID_W, D_FF), lambda i: (2, jnp.minimum((i + 1) * per, nt * per - 1), 0))]
        c_args += [c, c]
    return pl.pallas_call(
        functools.partial(_ffn_out_kernel, seq_tiles=seq_tiles // scale, grid_conv=grid_conv),
        grid=(nt,),
        in_specs=c_specs + [row(D_FF), row(D_MODEL),
                            pl.BlockSpec((None, 1, 6 * D_MODEL), lambda i: (mod_of_tile(i * scale), 0, 0)),
                            _full(w_down.shape), _full((1, D_MODEL))],
        out_specs=row(D_MODEL),
        out_shape=jax.ShapeDtypeStruct((t, D_MODEL), F32),
        compiler_params=_params("parallel"),
        name="ffn_out",
    )(*c_args, up, x1, mod, w_down, final_w)


def _block(x, seq_len, mod, mod_of_tile, s_hgrn, s_rwkv, grid_conv, w):
    seq_tiles = seq_len // ROW_TILE
    u_a, u_b = _in_proj(x, mod, w["norm_mix_w"], w["w_in_a"], w["w_in_b"], w["rwkv_conv"], seq_tiles, mod_of_tile)
    o_a, s_h = _hgrn_mixer(u_a, w["hgrn_lb"], w["hgrn_norm_w"], s_hgrn, seq_len)
    o_b, s_r = _rwkv_mixer(u_b, w["rwkv"], s_rwkv, seq_len)
    x1, conv_sums, up = _mix_out(o_a, o_b, x, mod, w["w_out_a"], w["w_out_b"], w["norm_ffn_w"], w["ffn_w_gate"],
                                 w["ffn_w_up"], w["ffn_conv"], w["ffn_conv_b"], seq_len, grid_conv, mod_of_tile)
    y = _ffn_out(conv_sums, up, x1, mod, w["ffn_w_down"], w["final_norm_w"], seq_tiles, grid_conv, mod_of_tile)
    return y, s_h, s_r


def _place_rows(w, start, total):
    return jnp.zeros((total, w.shape[1]), w.dtype).at[start:start + w.shape[0]].set(w)


def kernel(x_prompt, x_sample, state_hgrn, state_rwkv, c, c_ctx, ada_w, ada_b, norm_mix_w, w_in, hgrn_lb, hgrn_norm_w, rwkv_conv, rwkv_w0, rwkv_w2, rwkv_a0, rwkv_a2, rwkv_g2, rwkv_k_k, rwkv_k_a, rwkv_r_k, rwkv_ln_w, rwkv_ln_b, w_out, norm_ffn_w, ffn_w_gate, ffn_w_up, ffn_conv, ffn_conv_b, ffn_w_down, final_norm_w):
    assert w_in.shape[0] == 1, "one trunk layer"
    b_ctx, t_ctx, _ = x_prompt.shape
    b_lat, t_lat, _ = x_sample.shape
    row = lambda v: v.reshape(1, -1)
    pad_cols = P_B_PAD - P_B
    w_dec = jnp.concatenate([_place_rows(rwkv_w2[0, 0], 0, LORA_COLS), _place_rows(rwkv_w2[0, 1], LORA_W, LORA_COLS)], axis=1)
    pairs = N_HEADS_B // 2
    w_dec = w_dec.reshape(LORA_COLS, 2, pairs, LANES).transpose(2, 0, 1, 3).reshape(pairs, LORA_COLS, 2 * LANES)
    w_dec = jnp.stack(_split(w_dec), axis=1)
    weights = dict(
        norm_mix_w=row(norm_mix_w[0]),
        w_in_a=w_in[0, :, :P_A].astype(BF16),
        w_in_b=jnp.pad(w_in[0, :, P_A:], ((0, 0), (0, pad_cols))).astype(BF16),
        rwkv_conv=jnp.pad(rwkv_conv[0], ((0, 0), (0, pad_cols))),
        hgrn_lb=hgrn_lb,
        hgrn_norm_w=row(hgrn_norm_w[0]),
        rwkv=dict(w0=rwkv_w0[0], a0=row(rwkv_a0[0]), k_k=row(rwkv_k_k[0]), k_a=row(rwkv_k_a[0]), r_k=row(rwkv_r_k[0]),
                  ln_w=row(rwkv_ln_w[0]), ln_b=row(rwkv_ln_b[0]), w_dec=w_dec,
                  w_a=_place_rows(rwkv_a2[0], 2 * LORA_W, LORA_COLS),
                  w_g=_place_rows(rwkv_g2[0], 2 * LORA_W + LORA_A, LORA_COLS)),
        w_out_a=w_out[0, :WIDTH_A].astype(BF16),
        w_out_b=w_out[0, WIDTH_A:].astype(BF16),
        norm_ffn_w=row(norm_ffn_w[0]),
        ffn_w_gate=ffn_w_gate[0].astype(BF16),
        ffn_w_up=ffn_w_up[0].astype(BF16),
        ffn_conv=ffn_conv[0].reshape(9, D_FF),
        ffn_conv_b=row(ffn_conv_b[0]),
        ffn_w_down=ffn_w_down[0].astype(BF16),
        final_norm_w=row(final_norm_w),
    )
    cvec = jnp.concatenate([c_ctx[None, :], c, jnp.zeros((8 - 1 - b_lat, D_MODEL), F32)], axis=0)
    mod = _modulation(cvec, ada_w, ada_b).reshape(8, 1, 6 * D_MODEL)

    yp, s_h, s_r = _block(x_prompt.reshape(b_ctx * t_ctx, D_MODEL), t_ctx, mod, lambda i: 0, None, None, False, weights)
    lat_tiles = t_lat // ROW_TILE
    ys, _, _ = _block(x_sample.reshape(b_lat * t_lat, D_MODEL), t_lat, mod, lambda i: 1 + i // lat_tiles,
                      state_hgrn, state_rwkv, True, weights)
    y_prompt = yp.reshape(b_ctx, t_ctx, D_MODEL)
    y_sample = ys.reshape(b_lat, t_lat, D_MODEL)
    return (y_prompt, y_sample, s_h, s_r)
```

```python
import functools

import numpy as np
import jax
import jax.numpy as jnp
from jax import lax
from jax.experimental import pallas as pl
from jax.experimental.pallas import tpu as pltpu

F32 = jnp.float32
BF16 = jnp.bfloat16

D_MODEL = 1024
GRID_W = 64
WIDTH_A = 512
HEAD_A = 128
N_HEADS_A = 4
WIDTH_B = 512
HEAD_B = 64
N_HEADS_B = 8
LORA_W = 32
LORA_A = 32
LORA_G = 96
D_FF = 2816
RMS_EPS = 1e-6
GN_EPS = 64e-5
DECAY_SCALE = 0.6065306597
P_A = 5 * WIDTH_A
P_B = 3 * WIDTH_B + 2 * LORA_W + LORA_A + LORA_G
LORA_COLS = 256
P_B_PAD = 3 * WIDTH_B + LORA_COLS

CHUNK = 64
LANES = 128
SUBLANES = 8
ROW_TILE = 256
FFN_OUT_ROWS = 512
FFN_COL_BLOCK = 256
VMEM_LIMIT = 56 * 1024 * 1024

NN = (((1,), (0,)), ((), ()))
NT = (((1,), (1,)), ((), ()))
TN = (((0,), (0,)), ((), ()))


def _dot(a, b, dims=NN):
    return lax.dot_general(a, b, dims, preferred_element_type=F32)


def _split(a):
    hi = a.astype(BF16)
    lo = (a - hi.astype(F32)).astype(BF16)
    return hi, lo


def _mm(a, b, dims=NN, passes=1):
    if passes == 1:
        return _dot(a.astype(BF16), b.astype(BF16), dims)
    ah, al = _split(a)
    bh, bl = _split(b)
    return _dot(ah, bh, dims) + (_dot(ah, bl, dims) + _dot(al, bh, dims))


def _mm_exact_lhs(m, x):
    n = x.shape[1]
    both = _dot(m, jnp.concatenate(_split(x), axis=1))
    return both[:, :n] + both[:, n:]


def _sigmoid(x):
    return 0.5 * jnp.tanh(0.5 * x) + 0.5


def _silu(x):
    return x * _sigmoid(x)


def _hgrn_level_consts(c, rev):
    t = np.arange(c)
    mats = [(t[None, :] <= t[:, None]).astype(np.float32)]
    masks = [np.eye(c, dtype=np.float32)]
    n = 1
    while n < c:
        blk, half = t // (2 * n), (t // n) % 2
        mid = blk * 2 * n + n
        m = np.zeros((c, c), np.float32)
        for row in range(c):
            if half[row] == 1:
                m[row, mid[row]:row + 1] = 1.0
            else:
                m[row, row + 1:mid[row]] = 1.0
        mats.append(m)
        masks.append(((blk[:, None] == blk[None, :]) & (half[:, None] == 1) & (half[None, :] == 0))
                     .astype(np.float32))
        n *= 2
    if rev:
        mats = [m[::-1, ::-1] for m in mats]
        masks = [m[::-1, ::-1] for m in masks]
    return np.concatenate(mats, 0), np.concatenate(masks, 0)


def _hgrn_consts(c):
    mf, kf = _hgrn_level_consts(c, False)
    mb, kb = _hgrn_level_consts(c, True)
    mat_levels = 1 + int(np.log2(SUBLANES))
    return np.stack([mf, mb])[:, :mat_levels * c], np.stack([kf, kb])


def _rwkv_consts(c):
    t = np.arange(c)
    out = []
    for rev in (False, True):
        incl = (t[None, :] <= t[:, None]) if not rev else (t[None, :] >= t[:, None])
        strict = (t[None, :] < t[:, None]) if not rev else (t[None, :] > t[:, None])
        z = np.zeros((c, c), bool)
        cum = np.block([[incl, z], [z, z]])
        out.append(np.stack([cum, np.block([[strict, z], [z, strict]]), np.block([[incl, z], [z, incl]])]))
    return np.stack(out).astype(np.float32)


def _each(fn, *cols):
    return [fn(*args) for args in zip(*cols)]


def _interleave(*gens):
    results = [None] * len(gens)
    live = dict(enumerate(gens))
    while live:
        for i in list(live):
            try:
                next(live[i])
            except StopIteration as stop:
                results[i] = stop.value
                del live[i]
    return results


def _pair_exponent(b, n, rev):
    parts = []
    for base in range(0, b.shape[0], 2 * n):
        first, second = b[base:base + n], b[base + n:base + 2 * n]
        if rev:
            r = jnp.broadcast_to(b[base + n:base + n + 1], first.shape)
            parts += [first - r, r - second]
        else:
            r = jnp.broadcast_to(b[base + n - 1:base + n], first.shape)
            parts += [r - first, second - r]
    return jnp.concatenate(parts, axis=0)


def _hgrn_chunks_local(units):
    q, k, v, g, m, masks, rev = (list(col) for col in zip(*units))
    c = q[0].shape[0]
    levels = masks[0].shape[0] // c
    mat_levels = m[0].shape[0] // c
    e_all = _each(_mm_exact_lhs, m, g)
    yield
    b = [x[:c] for x in e_all]
    edge = _each(lambda x, rv: x[0:1] if rv else x[c - 1:c], b, rev)
    kv = _each(lambda vv, kk, ed, bb: _mm(vv, kk * jnp.exp(ed - bb), TN), v, k, edge, b)
    yield
    sc = _each(lambda mk, a, kk: mk[:c] * _mm(a, kk, NT), masks, q, k)
    yield
    for l in range(1, levels):
        if l < mat_levels:
            e = _each(lambda x: jnp.exp(x[l * c:(l + 1) * c]), e_all)
        else:
            e = _each(lambda bb, rv: jnp.exp(_pair_exponent(bb, 1 << (l - 1), rv)), b, rev)
        sc = _each(lambda s, mk, a, kk, ee: s + mk[l * c:(l + 1) * c] * _mm(a * ee, kk * ee, NT), sc, masks, q, k, e)
        yield
    intra = _each(_mm, sc, v)
    yield
    return [(ii, qq * jnp.exp(bb), kvv, jnp.exp(ed)) for ii, qq, bb, kvv, ed in zip(intra, q, b, kv, edge)]


def _hgrn_chunk_state(st, part):
    intra, qb, kv, decay = part
    return intra + _mm(qb, st, NT), st * decay + kv


def _stack_heads(x, lane_head):
    return jnp.concatenate([jnp.where(lane_head == 0, x, 0.0), jnp.where(lane_head == 1, x, 0.0)], axis=0)


RWKV_PASSES_SCORE = 1
RWKV_PASSES_SOLVE = 1
RWKV_PASSES_STATE = 1
RWKV_LOCAL_PARTS = 7
RWKV_UNITS = 8
HGRN_UNITS = 16
MIXER_ROW_BLOCK = 256


def _rwkv_chunks_local(units):
    c = units[0][0].shape[0]
    lane_head = lax.broadcasted_iota(jnp.int32, (c, LANES), 1) // HEAD_B
    st = lambda x: _stack_heads(x, lane_head)
    r, k, v, kk, beta, lw, consts, rev = (list(col) for col in zip(*units))
    strict = [cs[1] for cs in consts]
    incl = [cs[2] for cs in consts]
    g = _each(lambda cs, x: _mm_exact_lhs(cs[0][:c, :c].astype(BF16), x), consts, lw)
    edge = _each(lambda x, rv: x[0:1] if rv else x[c - 1:c], g, rev)
    yield
    kg = _each(lambda x, gg, l: st(x * jnp.exp(gg - l)), kk, g, lw)
    rg = _each(lambda x, gg: st(x * jnp.exp(gg)), r, g)
    e_out = _each(lambda gg: jnp.exp(-gg), g)
    bi = _each(lambda x, e: st(x * e), beta, e_out)
    ki = _each(lambda x, e: st(x * e), k, e_out)
    vs = _each(st, v)
    a_all = _each(lambda a, b, cc, d: _mm(jnp.concatenate([a, b], 0), jnp.concatenate([cc, d], 0), NT, RWKV_PASSES_SCORE),
                  kg, rg, bi, ki)
    yield
    n = _each(lambda m, a: m * a[:2 * c, :2 * c], strict, a_all)
    a_ak = _each(lambda m, a: m * a[:2 * c, 2 * c:], strict, a_all)
    a_rb = _each(lambda m, a: m * a[2 * c:, :2 * c], incl, a_all)
    a_rk = _each(lambda m, a: m * a[2 * c:, 2 * c:], incl, a_all)
    av = _each(lambda a, b, x: _mm(jnp.concatenate([a, b], 0), x, passes=RWKV_PASSES_SCORE), a_ak, a_rk, vs)
    yield
    eye = (lax.broadcasted_iota(jnp.int32, (2 * c, 2 * c), 0) == lax.broadcasted_iota(jnp.int32, (2 * c, 2 * c), 1)).astype(F32)
    tinv = _each(lambda a: eye - a, n)
    power = _each(lambda a: _mm(a, a, passes=RWKV_PASSES_SOLVE), n)
    yield
    span = 4
    while span < c:
        both = _each(lambda p, t: _mm(jnp.concatenate([p, t], axis=0), p, passes=RWKV_PASSES_SOLVE), power, tinv)
        power = [b[:2 * c] for b in both]
        tinv = _each(lambda t, b: t + b[2 * c:], tinv, both)
        span *= 2
        yield
    tinv = _each(lambda p, t: t + _mm(t, p, passes=RWKV_PASSES_SOLVE), power, tinv)
    yield
    x = _each(lambda t, a, b: _mm(t, jnp.concatenate([a, b[:2 * c]], axis=1), passes=RWKV_PASSES_SOLVE), tinv, kg, av)
    yield
    e_edge = _each(lambda ed, gg: jnp.exp(ed - gg), edge, g)
    kv = _each(lambda a, b, e: _mm(a, st(b * e), TN, RWKV_PASSES_STATE), vs, k, e_edge)
    yield
    bgc = _each(lambda b, e: st(b * e), beta, e_edge)
    return [((xx[:, :LANES], rr, xx[:, LANES:], ab, a[2 * c:], bg, kvv), jnp.exp(ed))
            for xx, rr, ab, a, bg, kvv, ed in zip(x, rg, a_rb, av, bgc, kv, edge)]


def _rwkv_chunks_state(states, parts, decays):
    w1, rg, x2, a_rb, y0, bgc, kv = (list(col) for col in zip(*parts))
    c2 = w1[0].shape[0]
    hs = _each(lambda a, b, s: _mm(jnp.concatenate([a, b], 0), s, NT, RWKV_PASSES_STATE), w1, rg, states)
    yield
    u = _each(lambda h, x: -(h[:c2] + x), hs, x2)
    s = _each(lambda s_, d, uu, b, kv_: s_ * d + _mm(uu, b, TN, RWKV_PASSES_STATE) + kv_, states, decays, u, bgc, kv)
    yield
    y = _each(lambda h, y_, a, uu: h[c2:] + y_ + _mm(a, uu, passes=RWKV_PASSES_STATE), hs, y0, a_rb, u)
    yield
    return [yy[:c2 // 2] + yy[c2 // 2:] for yy in y], s


def _params(*semantics):
    return pltpu.CompilerParams(dimension_semantics=semantics, vmem_limit_bytes=VMEM_LIMIT)


def _full(shape):
    return pl.BlockSpec(shape, lambda *_: (0,) * len(shape))


def _modulated_norm(x, norm_w, scale, shift):
    y = x * lax.rsqrt(jnp.mean(x * x, axis=-1, keepdims=True) + RMS_EPS)
    return (y * norm_w) * (1.0 + scale) + shift


def _mod_kernel(c_ref, w_ref, b_ref, o_ref):
    o_ref[...] = _mm(_silu(c_ref[...]), w_ref[...], passes=3) + b_ref[...]


def _modulation(cvec, ada_w, ada_b):
    n = ada_w.shape[2]
    tn = n // 4
    return pl.pallas_call(
        _mod_kernel,
        grid=(n // tn,),
        in_specs=[_full(cvec.shape), pl.BlockSpec((None, D_MODEL, tn), lambda j: (0, 0, j)),
                  pl.BlockSpec((1, tn), lambda j: (0, j))],
        out_specs=pl.BlockSpec((cvec.shape[0], tn), lambda j: (0, j)),
        out_shape=jax.ShapeDtypeStruct((cvec.shape[0], n), F32),
        compiler_params=_params("arbitrary"),
        name="modulation",
    )(cvec, ada_w, ada_b)


def _in_proj_kernel(x_ref, xp_ref, xn_ref, mod_ref, nw_ref, wa_ref, wb_ref, cw_ref, ua_ref, ub_ref, *, seq_tiles):
    i = pl.program_id(0)
    mod = mod_ref[...]
    shift, scale = mod[:, 0:D_MODEL], mod[:, D_MODEL:2 * D_MODEL]
    nw = nw_ref[...]
    h = _modulated_norm(x_ref[...], nw, scale, shift).astype(BF16)
    halo = jnp.concatenate([xp_ref[...], xn_ref[...]], axis=0)
    hh = _modulated_norm(halo, nw, scale, shift).astype(BF16)
    ub_all = _dot(jnp.concatenate([h, hh], axis=0), wb_ref[...])
    ua_ref[...] = _dot(h, wa_ref[...])
    n = h.shape[0]
    ub, ubh = ub_all[:n], ub_all[n:]
    first = (i % seq_tiles) == 0
    last = (i % seq_tiles) == seq_tiles - 1
    prev_row = jnp.where(first, 0.0, ubh[7:8])
    next_row = jnp.where(last, 0.0, ubh[8:9])
    rows = lax.broadcasted_iota(jnp.int32, ub.shape, 0)
    below = jnp.where(rows == 0, prev_row, pltpu.roll(ub, 1, 0))
    above = jnp.where(rows == n - 1, next_row, pltpu.roll(ub, n - 1, 0))
    cw = cw_ref[...]
    ub_ref[...] = cw[0:1] * below + cw[1:2] * ub + cw[2:3] * above


def _in_proj(x, mod, norm_w, w_a, w_b, conv_w, seq_tiles, mod_of_tile):
    t = x.shape[0]
    nt = t // ROW_TILE
    sub = ROW_TILE // 8
    return pl.pallas_call(
        functools.partial(_in_proj_kernel, seq_tiles=seq_tiles),
        grid=(nt,),
        in_specs=[
            pl.BlockSpec((ROW_TILE, D_MODEL), lambda i: (i, 0)),
            pl.BlockSpec((8, D_MODEL), lambda i: (jnp.maximum(i * sub - 1, 0), 0)),
            pl.BlockSpec((8, D_MODEL), lambda i: (jnp.minimum((i + 1) * sub, nt * sub - 1), 0)),
            pl.BlockSpec((None, 1, 6 * D_MODEL), lambda i: (mod_of_tile(i), 0, 0)),
            _full((1, D_MODEL)), _full(w_a.shape), _full(w_b.shape), _full(conv_w.shape),
        ],
        out_specs=[pl.BlockSpec((ROW_TILE, P_A), lambda i: (i, 0)), pl.BlockSpec((ROW_TILE, P_B_PAD), lambda i: (i, 0))],
        out_shape=[jax.ShapeDtypeStruct((t, P_A), F32), jax.ShapeDtypeStruct((t, P_B_PAD), F32)],
        compiler_params=_params("parallel"),
        name="in_proj",
    )(x, x, x, mod, norm_w, w_a, w_b, conv_w)


def _hgrn_kernel(*refs, ns, nc, group, block, zero_state):
    q_ref, i_ref, zf_ref, zb_ref, g_ref, lb_ref, nw_ref = refs[:7]
    s0_ref = None if zero_state else refs[7]
    m_ref, mask_ref, o_ref, sout_ref, acc_s, st_ref = refs[-6:]
    c = CHUNK
    seq_len = nc * c
    chains = [(s, d) for s in range(ns) for d in range(2)]
    p0, p1 = lb_ref[0], lb_ref[1]
    mx = jnp.maximum(p0, p1)
    e0, e1 = jnp.exp(p0 - mx), jnp.exp(p1 - mx)
    lb = e0 / (e0 + e1)
    for ch, (s, d) in enumerate(chains):
        st_ref[ch] = jnp.zeros((HEAD_A, HEAD_A), F32) if zero_state else s0_ref[s, d].T

    def rows_of(gi, j, s, d):
        cf = gi * group + j
        ci = cf if d == 0 else nc - 1 - cf
        return pl.ds(pl.multiple_of(s * seq_len + ci * c, c), c)

    def scan_group(gi, carry):
        units = []
        for j in range(group):
            for s, d in chains:
                rows = rows_of(gi, j, s, d)
                lo = lb[d:d + 1]
                f = lo + (1.0 - lo) * _sigmoid((zf_ref, zb_ref)[d][rows, :])
                units.append((_silu(q_ref[rows, :]), 1.0 - f, i_ref[rows, :], jnp.log(f), m_ref[d], mask_ref[d], d == 1))
        parts, = _interleave(_hgrn_chunks_local(units))
        st = [st_ref[ch] for ch in range(len(chains))]
        for j in range(group):
            for ch, (s, d) in enumerate(chains):
                o, st[ch] = _hgrn_chunk_state(st[ch], parts[j * len(chains) + ch])
                acc_s[rows_of(gi, j, s, d), :] += o
        for ch, state in enumerate(st):
            st_ref[ch] = state
        return carry

    acc_s[...] = jnp.zeros(acc_s.shape, F32)
    lax.fori_loop(0, nc // group, scan_group, 0)

    def finish(bi, carry):
        rows = pl.ds(pl.multiple_of(bi * block, block), block)
        o = acc_s[rows, :]
        o = o * lax.rsqrt(jnp.mean(o * o, axis=-1, keepdims=True) + RMS_EPS) * nw_ref[...]
        o_ref[rows, :] = o * _silu(g_ref[rows, :])
        return carry

    lax.fori_loop(0, ns * seq_len // block, finish, 0, unroll=2 if ns * seq_len // block % 2 == 0 else 1)
    for ch, (s, d) in enumerate(chains):
        sout_ref[s, d] = st_ref[ch].T


def _hgrn_mixer(u_a, lb_raw, norm_w, s0, seq_len):
    t = u_a.shape[0]
    nb = t // seq_len
    nc = seq_len // CHUNK
    group = min(HGRN_UNITS // 2, nc)
    ns = max(1, min(HGRN_UNITS // (2 * group), nb))
    assert nc % group == 0 and nb % ns == 0
    rows = ns * seq_len
    mats, masks = _hgrn_consts(CHUNK)
    mats = jnp.asarray(mats, BF16)
    masks = jnp.asarray(masks, F32)
    col = lambda part: pl.BlockSpec((rows, HEAD_A), lambda b, h: (b, part * N_HEADS_A + h))
    state_spec = pl.BlockSpec((ns, 2, None, HEAD_A, HEAD_A), lambda b, h: (b, 0, h, 0, 0))
    state_in = ([], []) if s0 is None else ([state_spec], [s0])
    return pl.pallas_call(
        functools.partial(_hgrn_kernel, ns=ns, nc=nc, group=group, block=min(MIXER_ROW_BLOCK, rows),
                          zero_state=s0 is None),
        grid=(nb // ns, N_HEADS_A),
        in_specs=[col(0), col(1), col(2), col(3), col(4),
                  pl.BlockSpec((2, 2, HEAD_A), lambda b, h: (0, 0, h)),
                  pl.BlockSpec((1, HEAD_A), lambda b, h: (0, h)),
                  *state_in[0], _full(mats.shape), _full(masks.shape)],
        out_specs=[pl.BlockSpec((rows, HEAD_A), lambda b, h: (b, h)), state_spec],
        out_shape=[jax.ShapeDtypeStruct((t, WIDTH_A), F32), jax.ShapeDtypeStruct((nb, 2, N_HEADS_A, HEAD_A, HEAD_A), F32)],
        scratch_shapes=[pltpu.VMEM((rows, HEAD_A), F32), pltpu.VMEM((2 * ns, HEAD_A, HEAD_A), F32)],
        compiler_params=_params("parallel", "parallel"),
        name="hgrn2_scan",
    )(u_a, u_a, u_a, u_a, u_a, lb_raw, norm_w, *state_in[1], mats, masks)


def _mm_exact_rhs(x, m):
    n = x.shape[0]
    both = _dot(jnp.concatenate(_split(x), axis=0), m)
    return both[:n] + both[n:]


def _rwkv_kernel(*refs, ns, nc, group, block, zero_state):
    (r_ref, k_ref, v_ref, lora_ref, w0_ref, a0_ref, kkw_ref, ka_ref, rk_ref, lnw_ref, lnb_ref,
     wdec_ref, wa_ref, wg_ref) = refs[:14]
    s0_ref = None if zero_state else refs[14]
    (consts_ref, seg_ref, o_ref, sout_ref, lwf_s, lwb_s, kk_s, beta_s, kmod_s, y_s,
     loca_s, deca_s, locb_s, decb_s, st_ref) = refs[-15:]
    c = CHUNK
    seq_len = nc * c
    chains = [(s, d) for s in range(ns) for d in range(2)]
    seg = seg_ref[...]

    for ch, (s, d) in enumerate(chains):
        st_ref[ch] = jnp.zeros((LANES, LANES), F32)
        if not zero_state:
            st_ref[ch, 0:HEAD_B, 0:HEAD_B] = s0_ref[s, d, 0]
            st_ref[ch, HEAD_B:LANES, HEAD_B:LANES] = s0_ref[s, d, 1]

    def block_of(bi):
        return pl.ds(bi * block, block)

    edge_chunks = list(range(group)) + list(range(nc - group, nc))
    edge_blocks = sorted({(s * seq_len + ci * c) // block for s in range(ns) for ci in edge_chunks})
    inner_blocks = [bi for bi in range(ns * seq_len // block) if bi not in edge_blocks]

    def one_per_stage(fn, blocks):
        for bi in blocks:
            fn(bi)
            yield

    def prepare(bi):
        rows = block_of(bi)
        lora = lora_ref[rows, :]
        th, tl = _split(jnp.tanh(lora))
        w_hi, w_lo = wdec_ref[0], wdec_ref[1]
        dec = _dot(th, w_hi) + (_dot(th, w_lo) + _dot(tl, w_hi))
        w0 = w0_ref[...]
        lwf_s[rows, :] = -DECAY_SCALE * _sigmoid(w0[0:1] + dec[:, :LANES])
        lwb_s[rows, :] = -DECAY_SCALE * _sigmoid(w0[1:2] + dec[:, LANES:])
        a = _sigmoid(a0_ref[...] + _mm(lora, wa_ref[...]))
        k = k_ref[rows, :]
        kk = k * kkw_ref[...]
        kk = kk * lax.rsqrt(_mm(kk * kk, seg) + 1e-12)
        kk_s[rows, :] = kk
        beta_s[rows, :] = kk * a
        kmod_s[rows, :] = k * (1.0 + (a - 1.0) * ka_ref[...])

    for bi in edge_blocks:
        prepare(bi)

    def rows_of(gi, j, s, d):
        cf = gi * group + j
        ci = cf if d == 0 else nc - 1 - cf
        return pl.ds(pl.multiple_of(s * seq_len + ci * c, c), c)

    def local_stage(gi, loc, dec):
        units = []
        for j in range(group):
            for s, d in chains:
                rows = rows_of(gi, j, s, d)
                units.append((r_ref[rows, :], kmod_s[rows, :], v_ref[rows, :], kk_s[rows, :], beta_s[rows, :],
                              (lwf_s, lwb_s)[d][rows, :], consts_ref[d], d == 1))
        results = yield from _rwkv_chunks_local(units)
        for u, (parts, decay) in enumerate(results):
            j, ch = divmod(u, len(chains))
            for idx, part in enumerate(parts):
                loc[ch, j, idx] = part
            dec[ch, j] = decay

    def state_stage(gi, loc, dec):
        states = [st_ref[ch] for ch in range(len(chains))]
        for j in range(group):
            parts = [[loc[ch, j, idx] for idx in range(RWKV_LOCAL_PARTS)] for ch in range(len(chains))]
            ys, states = yield from _rwkv_chunks_state(states, parts, [dec[ch, j] for ch in range(len(chains))])
            for y, (s, d) in zip(ys, chains):
                y_s[rows_of(gi, j, s, d), :] += y
        for ch, state in enumerate(states):
            st_ref[ch] = state

    y_s[...] = jnp.zeros(y_s.shape, F32)
    ng = nc // group
    _interleave(local_stage(0, loca_s, deca_s), one_per_stage(prepare, inner_blocks))

    def two_groups(p, carry):
        _interleave(local_stage(2 * p + 1, locb_s, decb_s), state_stage(2 * p, loca_s, deca_s))
        _interleave(local_stage(2 * p + 2, loca_s, deca_s), state_stage(2 * p + 1, locb_s, decb_s))
        return carry

    lax.fori_loop(0, ng // 2 - 1, two_groups, 0)
    _interleave(local_stage(ng - 1, locb_s, decb_s), state_stage(ng - 2, loca_s, deca_s))
    def finish(bi):
        rows = block_of(bi)
        y = y_s[rows, :]
        inv_n = 1.0 / HEAD_B
        mu = _mm_exact_rhs(y, seg) * inv_n
        dy = y - mu
        var = _mm(dy * dy, seg) * inv_n
        yn = dy * lax.rsqrt(var + GN_EPS) * lnw_ref[...] + lnb_ref[...]
        v = v_ref[rows, :]
        bonus = _mm(r_ref[rows, :] * kmod_s[rows, :] * rk_ref[...], seg) * v
        g = _mm(_sigmoid(lora_ref[rows, :]), wg_ref[...])
        o_ref[rows, :] = (yn + bonus) * g

    _interleave(state_stage(ng - 1, locb_s, decb_s), one_per_stage(finish, inner_blocks))
    for bi in edge_blocks:
        finish(bi)
    for ch, (s, d) in enumerate(chains):
        sout_ref[s, d, 0] = st_ref[ch, 0:HEAD_B, 0:HEAD_B]
        sout_ref[s, d, 1] = st_ref[ch, HEAD_B:LANES, HEAD_B:LANES]


def _rwkv_mixer(u_b, p, s0, seq_len):
    t = u_b.shape[0]
    nb = t // seq_len
    nc = seq_len // CHUNK
    group = max(1, min(RWKV_UNITS // 2, nc // 4))
    ns = max(1, min(RWKV_UNITS // (2 * group), nb))
    assert nc % (2 * group) == 0 and nb % ns == 0
    rows = ns * seq_len
    block = min(MIXER_ROW_BLOCK, rows)
    pairs = N_HEADS_B // 2
    consts = jnp.asarray(_rwkv_consts(CHUNK), F32)
    lane = np.arange(LANES) // HEAD_B
    seg = jnp.asarray(lane[:, None] == lane[None, :], BF16)
    col = lambda part: pl.BlockSpec((rows, LANES), lambda b, h: (b, part * pairs + h))
    vec = lambda n: pl.BlockSpec((n, LANES), lambda b, h: (0, h))
    state_spec = pl.BlockSpec((ns, 2, 2, HEAD_B, HEAD_B), lambda b, h: (b, 0, h, 0, 0))
    scr = lambda: pltpu.VMEM((rows, LANES), F32)
    loc = lambda: pltpu.VMEM((2 * ns, group, RWKV_LOCAL_PARTS, LANES, LANES), F32)
    dec = lambda: pltpu.VMEM((2 * ns, group, 1, LANES), F32)
    state_in = ([], []) if s0 is None else ([state_spec], [s0])
    return pl.pallas_call(
        functools.partial(_rwkv_kernel, ns=ns, nc=nc, group=group, block=block, zero_state=s0 is None),
        grid=(nb // ns, pairs),
        in_specs=[col(0), col(1), col(2),
                  pl.BlockSpec((rows, LORA_COLS), lambda b, h: (b, 3 * WIDTH_B // LORA_COLS)),
                  vec(2), vec(1), vec(1), vec(1), vec(1), vec(1), vec(1),
                  pl.BlockSpec((None, 2, LORA_COLS, 2 * LANES), lambda b, h: (h, 0, 0, 0)),
                  pl.BlockSpec((LORA_COLS, LANES), lambda b, h: (0, h)),
                  pl.BlockSpec((LORA_COLS, LANES), lambda b, h: (0, h)),
                  *state_in[0], _full(consts.shape), _full(seg.shape)],
        out_specs=[pl.BlockSpec((rows, LANES), lambda b, h: (b, h)), state_spec],
        out_shape=[jax.ShapeDtypeStruct((t, WIDTH_B), F32), jax.ShapeDtypeStruct((nb, 2, N_HEADS_B, HEAD_B, HEAD_B), F32)],
        scratch_shapes=[scr(), scr(), scr(), scr(), scr(), scr(), loc(), dec(), loc(), dec(),
                        pltpu.VMEM((2 * ns, LANES, LANES), F32)],
        compiler_params=_params("parallel", "parallel"),
        name="rwkv7_scan",
    )(u_b, u_b, u_b, u_b, p["w0"], p["a0"], p["k_k"], p["k_a"], p["r_k"], p["ln_w"], p["ln_b"],
      p["w_dec"], p["w_a"], p["w_g"], *state_in[1], consts, seg)


def _mix_out_kernel(oa_ref, ob_ref, x_ref, mod_ref, woa_ref, wob_ref, nw_ref, wg_ref, wu_ref, cw_ref, cb_ref,
                    x1_ref, c_ref, u_ref, *, grid_conv):
    mod = mod_ref[...]
    gate1 = mod[:, 2 * D_MODEL:3 * D_MODEL]
    shift2, scale2 = mod[:, 3 * D_MODEL:4 * D_MODEL], mod[:, 4 * D_MODEL:5 * D_MODEL]
    mix = _dot(oa_ref[...].astype(BF16), woa_ref[...]) + _dot(ob_ref[...].astype(BF16), wob_ref[...])
    x1 = x_ref[...] + gate1 * mix
    x1_ref[...] = x1
    h = _modulated_norm(x1, nw_ref[...], scale2, shift2).astype(BF16)
    n = h.shape[0]
    width = GRID_W if grid_conv else n
    col = lax.broadcasted_iota(jnp.int32, (width, FFN_COL_BLOCK), 0)
    blocks = [slice(j * FFN_COL_BLOCK, (j + 1) * FFN_COL_BLOCK) for j in range(D_FF // FFN_COL_BLOCK)]
    g_next = _dot(h, wg_ref[:, blocks[0]])
    for j, cols in enumerate(blocks):
        g_all = g_next
        u_ref[:, cols] = _dot(h, wu_ref[:, cols]).astype(BF16)
        if j + 1 < len(blocks):
            g_next = _dot(h, wg_ref[:, blocks[j + 1]])
        cw = cw_ref[:, cols].astype(BF16)
        bias = cb_ref[:, cols].astype(BF16)
        for r0 in range(0, n, width):
            g = g_all[r0:r0 + width]
            left = jnp.where(col == 0, 0.0, pltpu.roll(g, 1, 0)).astype(BF16)
            right = jnp.where(col == width - 1, 0.0, pltpu.roll(g, width - 1, 0)).astype(BF16)
            g = g.astype(BF16)
            for slot, dr in enumerate((0, 1, 2) if grid_conv else (1,)):
                taps = cw[3 * dr:3 * dr + 1] * left + cw[3 * dr + 1:3 * dr + 2] * g + cw[3 * dr + 2:3 * dr + 3] * right
                c_ref[slot, r0:r0 + width, cols] = taps + bias if dr == 1 else taps


def _mix_out(o_a, o_b, x, mod, w_out_a, w_out_b, norm_w, w_gate, w_up, conv_w, conv_b, grid_conv, mod_of_tile):
    t = x.shape[0]
    nr = 3 if grid_conv else 1
    row = lambda n: pl.BlockSpec((ROW_TILE, n), lambda i: (i, 0))
    return pl.pallas_call(
        functools.partial(_mix_out_kernel, grid_conv=grid_conv),
        grid=(t // ROW_TILE,),
        in_specs=[row(WIDTH_A), row(WIDTH_B), row(D_MODEL),
                  pl.BlockSpec((None, 1, 6 * D_MODEL), lambda i: (mod_of_tile(i), 0, 0)),
                  _full(w_out_a.shape), _full(w_out_b.shape), _full((1, D_MODEL)), _full(w_gate.shape), _full(w_up.shape),
                  _full(conv_w.shape), _full(conv_b.shape)],
        out_specs=[row(D_MODEL), pl.BlockSpec((nr, ROW_TILE, D_FF), lambda i: (0, i, 0)), row(D_FF)],
        out_shape=[jax.ShapeDtypeStruct((t, D_MODEL), F32), jax.ShapeDtypeStruct((nr, t, D_FF), BF16),
                   jax.ShapeDtypeStruct((t, D_FF), BF16)],
        compiler_params=_params("parallel"),
        name="mix_out_ffn_in",
    )(o_a, o_b, x, mod, w_out_a, w_out_b, norm_w, w_gate, w_up, conv_w, conv_b)


def _gelu_tanh(x):
    half = 0.5 * x
    return half + half * jnp.tanh(x * (0.7978845608028654 + (0.7978845608028654 * 0.044715) * (x * x)))


def _ffn_out_kernel(*refs, seq_tiles, grid_conv):
    if grid_conv:
        c_ref, ca_ref, cb_ref, u_ref, x1_ref, mod_ref, wd_ref, fw_ref, y_ref = refs
        i = pl.program_id(0)
        top = (i % seq_tiles) == 0
        bottom = (i % seq_tiles) == seq_tiles - 1
        n = c_ref.shape[1]
        zero = jnp.zeros((), BF16)
        up_rows = jnp.concatenate([jnp.where(top, zero, ca_ref[...]), c_ref[0, :n - GRID_W, :]], axis=0)
        down_rows = jnp.concatenate([c_ref[2, GRID_W:, :], jnp.where(bottom, zero, cb_ref[...])], axis=0)
        gt = (c_ref[1] + up_rows + down_rows).astype(F32)
    else:
        c_ref, u_ref, x1_ref, mod_ref, wd_ref, fw_ref, y_ref = refs
        gt = c_ref[0].astype(F32)
    act = (_gelu_tanh(gt) * u_ref[...].astype(F32)).astype(BF16)
    mod = mod_ref[...]
    gate2 = mod[:, 5 * D_MODEL:6 * D_MODEL]
    x2 = x1_ref[...] + gate2 * _dot(act, wd_ref[...])
    y_ref[...] = x2 * lax.rsqrt(jnp.mean(x2 * x2, axis=-1, keepdims=True) + RMS_EPS) * fw_ref[...]


def _ffn_out(c, up, x1, mod, w_down, final_w, seq_tiles, grid_conv, mod_of_tile):
    t = x1.shape[0]
    tile = FFN_OUT_ROWS
    scale = tile // ROW_TILE
    nt = t // tile
    per = tile // GRID_W
    row = lambda n: pl.BlockSpec((tile, n), lambda i: (i, 0))
    c_specs = [pl.BlockSpec((c.shape[0], tile, D_FF), lambda i: (0, i, 0))]
    c_args = [c]
    if grid_conv:
        c_specs += [pl.BlockSpec((None, GRID_W, D_FF), lambda i: (0, jnp.maximum(i * per - 1, 0), 0)),
                    pl.BlockSpec((None, GRID_W, D_FF), lambda i: (2, jnp.minimum((i + 1) * per, nt * per - 1), 0))]
        c_args += [c, c]
    return pl.pallas_call(
        functools.partial(_ffn_out_kernel, seq_tiles=seq_tiles // scale, grid_conv=grid_conv),
        grid=(nt,),
        in_specs=c_specs + [row(D_FF), row(D_MODEL),
                            pl.BlockSpec((None, 1, 6 * D_MODEL), lambda i: (mod_of_tile(i * scale), 0, 0)),
                            _full(w_down.shape), _full((1, D_MODEL))],
        out_specs=row(D_MODEL),
        out_shape=jax.ShapeDtypeStruct((t, D_MODEL), F32),
        compiler_params=_params("parallel"),
        name="ffn_out",
    )(*c_args, up, x1, mod, w_down, final_w)


def _block(x, seq_len, mod, mod_of_tile, s_hgrn, s_rwkv, grid_conv, w):
    seq_tiles = seq_len // ROW_TILE
    u_a, u_b = _in_proj(x, mod, w["norm_mix_w"], w["w_in_a"], w["w_in_b"], w["rwkv_conv"], seq_tiles, mod_of_tile)
    o_a, s_h = _hgrn_mixer(u_a, w["hgrn_lb"], w["hgrn_norm_w"], s_hgrn, seq_len)
    o_b, s_r = _rwkv_mixer(u_b, w["rwkv"], s_rwkv, seq_len)
    x1, conv_sums, up = _mix_out(o_a, o_b, x, mod, w["w_out_a"], w["w_out_b"], w["norm_ffn_w"], w["ffn_w_gate"],
                                 w["ffn_w_up"], w["ffn_conv"], w["ffn_conv_b"], grid_conv, mod_of_tile)
    y = _ffn_out(conv_sums, up, x1, mod, w["ffn_w_down"], w["final_norm_w"], seq_tiles, grid_conv, mod_of_tile)
    return y, s_h, s_r


def _place_rows(w, start, total):
    return jnp.zeros((total, w.shape[1]), w.dtype).at[start:start + w.shape[0]].set(w)


def kernel(x_prompt, x_sample, state_hgrn, state_rwkv, c, c_ctx, ada_w, ada_b, norm_mix_w, w_in, hgrn_lb, hgrn_norm_w, rwkv_conv, rwkv_w0, rwkv_w2, rwkv_a0, rwkv_a2, rwkv_g2, rwkv_k_k, rwkv_k_a, rwkv_r_k, rwkv_ln_w, rwkv_ln_b, w_out, norm_ffn_w, ffn_w_gate, ffn_w_up, ffn_conv, ffn_conv_b, ffn_w_down, final_norm_w):
    assert w_in.shape[0] == 1, "one trunk layer"
    b_ctx, t_ctx, _ = x_prompt.shape
    b_lat, t_lat, _ = x_sample.shape
    row = lambda v: v.reshape(1, -1)
    pad_cols = P_B_PAD - P_B
    w_dec = jnp.concatenate([_place_rows(rwkv_w2[0, 0], 0, LORA_COLS), _place_rows(rwkv_w2[0, 1], LORA_W, LORA_COLS)], axis=1)
    pairs = N_HEADS_B // 2
    w_dec = w_dec.reshape(LORA_COLS, 2, pairs, LANES).transpose(2, 0, 1, 3).reshape(pairs, LORA_COLS, 2 * LANES)
    w_dec = jnp.stack(_split(w_dec), axis=1)
    weights = dict(
        norm_mix_w=row(norm_mix_w[0]),
        w_in_a=w_in[0, :, :P_A].astype(BF16),
        w_in_b=jnp.pad(w_in[0, :, P_A:], ((0, 0), (0, pad_cols))).astype(BF16),
        rwkv_conv=jnp.pad(rwkv_conv[0], ((0, 0), (0, pad_cols))),
        hgrn_lb=hgrn_lb,
        hgrn_norm_w=row(hgrn_norm_w[0]),
        rwkv=dict(w0=rwkv_w0[0], a0=row(rwkv_a0[0]), k_k=row(rwkv_k_k[0]), k_a=row(rwkv_k_a[0]), r_k=row(rwkv_r_k[0]),
                  ln_w=row(rwkv_ln_w[0]), ln_b=row(rwkv_ln_b[0]), w_dec=w_dec,
                  w_a=_place_rows(rwkv_a2[0], 2 * LORA_W, LORA_COLS),
                  w_g=_place_rows(rwkv_g2[0], 2 * LORA_W + LORA_A, LORA_COLS)),
        w_out_a=w_out[0, :WIDTH_A].astype(BF16),
        w_out_b=w_out[0, WIDTH_A:].astype(BF16),
        norm_ffn_w=row(norm_ffn_w[0]),
        ffn_w_gate=ffn_w_gate[0].astype(BF16),
        ffn_w_up=ffn_w_up[0].astype(BF16),
        ffn_conv=ffn_conv[0].reshape(9, D_FF),
        ffn_conv_b=row(ffn_conv_b[0]),
        ffn_w_down=ffn_w_down[0].astype(BF16),
        final_norm_w=row(final_norm_w),
    )
    cvec = jnp.concatenate([c_ctx[None, :], c, jnp.zeros((8 - 1 - b_lat, D_MODEL), F32)], axis=0)
    mod = _modulation(cvec, ada_w, ada_b).reshape(8, 1, 6 * D_MODEL)

    yp, s_h, s_r = _block(x_prompt.reshape(b_ctx * t_ctx, D_MODEL), t_ctx, mod, lambda i: 0, None, None, False, weights)
    lat_tiles = t_lat // ROW_TILE
    ys, _, _ = _block(x_sample.reshape(b_lat * t_lat, D_MODEL), t_lat, mod, lambda i: 1 + i // lat_tiles,
                      state_hgrn[:, 0], state_rwkv[:, 0], True, weights)
    y_prompt = yp.reshape(b_ctx, t_ctx, D_MODEL)
    y_sample = ys.reshape(b_lat, t_lat, D_MODEL)
    new_state_hgrn = s_h[:, None]
    new_state_rwkv = s_r[:, None]
    return (y_prompt, y_sample, new_state_hgrn, new_state_rwkv)
```
